```python
import jax, jax.numpy as jnp
from jax import lax
import numpy as np

D_MODEL = 1024
BATCH = 32
SEQ = 2048
DEPTH = 2
DEC_BATCH = 4
DEC_SEQ = 4096
PAST_LEN = 128

N_META = 16
BLOCK = 128
WINDOW = 128
ROPE_THETA = 500000.0
MLA_HEADS = D_MODEL // 64
MLA_NOPE = 64
MLA_ROPE = 32
MLA_V = 64
MLA_KV_RANK = 4 * MLA_NOPE
MLA_Q_RANK = 3 * MLA_KV_RANK
GQA_Q_HEADS = 16
GQA_KV_HEADS = 4
GQA_GROUP = GQA_Q_HEADS // GQA_KV_HEADS
GQA_HEAD_DIM = D_MODEL // GQA_Q_HEADS
GQA_ROT = GQA_HEAD_DIM // 4
D_FF = 4 * D_MODEL
N_MLA_LAYERS = (DEPTH + 1) // 2
N_GQA_LAYERS = DEPTH // 2
DN_ALPHA = (2.0 * DEPTH) ** 0.25
DN_BETA = (8.0 * DEPTH) ** -0.25
LN_EPS = 1e-5
RMS_EPS = 1e-6
NEG_INF = -1e30

kernel_name = 'hybrid_mla_swa_sink_encoder'


def layer_norm(x, g, b):
    xf = x.astype(jnp.float32)
    mu = jnp.mean(xf, axis=-1, keepdims=True)
    xc = xf - mu
    var = jnp.mean(xc * xc, axis=-1, keepdims=True)
    y = xc * lax.rsqrt(var + LN_EPS) * g.astype(jnp.float32) + b.astype(jnp.float32)
    return y.astype(x.dtype)


def rms_norm(x, g):
    xf = x.astype(jnp.float32)
    y = xf * lax.rsqrt(jnp.mean(xf * xf, axis=-1, keepdims=True) + RMS_EPS) * g.astype(jnp.float32)
    return y.astype(x.dtype)


def rope_tables(length, dim, dtype):
    pos = jnp.arange(length, dtype=jnp.float32)
    inv = ROPE_THETA ** (-jnp.arange(0, dim, 2, dtype=jnp.float32) / dim)
    ang = pos[:, None] * inv[None, :]
    return jnp.cos(ang).astype(dtype), jnp.sin(ang).astype(dtype)


def apply_rope(x, cos, sin):
    c = cos[None, :, None, :]
    s = sin[None, :, None, :]
    x1, x2 = jnp.split(x, 2, axis=-1)
    return jnp.concatenate([x1 * c - x2 * s, x2 * c + x1 * s], axis=-1)


def partial_rope(x, cos, sin):
    return jnp.concatenate([apply_rope(x[..., :GQA_ROT], cos, sin), x[..., GQA_ROT:]], axis=-1)


def mla_attend(qn, qr, kn, kr, v):
    scale = (MLA_NOPE + MLA_ROPE) ** -0.5
    s = jnp.einsum('bqhd,bkhd->bhqk', qn, kn) + jnp.einsum('bqhr,bkr->bhqk', qr, kr)
    p = jax.nn.softmax(s.astype(jnp.float32) * scale, axis=-1).astype(v.dtype)
    return jnp.einsum('bhqk,bkhd->bqhd', p, v)


def mla_mixer(x, cos, sin, w_in, g_q, w_uq, g_kv, w_ukv, w_o):
    B, L, _ = x.shape
    S = L - N_META
    nb = S // BLOCK
    c = x @ w_in
    cq = rms_norm(c[..., :MLA_Q_RANK], g_q)
    ckv = rms_norm(c[..., MLA_Q_RANK:MLA_Q_RANK + MLA_KV_RANK], g_kv)
    kr = apply_rope(c[..., MLA_Q_RANK + MLA_KV_RANK:][:, :, None, :], cos, sin)[:, :, 0, :]
    q = (cq @ w_uq).reshape(B, L, MLA_HEADS, MLA_NOPE + MLA_ROPE)
    qn = q[..., :MLA_NOPE]
    qr = apply_rope(q[..., MLA_NOPE:], cos, sin)
    kv = (ckv @ w_ukv).reshape(B, L, MLA_HEADS, MLA_NOPE + MLA_V)
    kn = kv[..., :MLA_NOPE]
    v = kv[..., MLA_NOPE:]
    out_meta = mla_attend(qn[:, :N_META], qr[:, :N_META], kn, kr, v)

    def to_blocks(t):
        return t[:, N_META:].reshape((B, nb, BLOCK) + t.shape[2:]).swapaxes(0, 1)

    out_real = lax.map(lambda qb: mla_attend(qb[0], qb[1], kn, kr, v), (to_blocks(qn), to_blocks(qr)))
    out_real = out_real.swapaxes(0, 1).reshape(B, S, MLA_HEADS, MLA_V)
    out = jnp.concatenate([out_meta, out_real], axis=1).reshape(B, L, MLA_HEADS * MLA_V)
    return out @ w_o


def sink_attend(q, k, v, mask, sink):
    s = jnp.einsum('bqkgd,bjkd->bkgqj', q, k).astype(jnp.float32) * (GQA_HEAD_DIM ** -0.5)
    s = jnp.where(mask, s, NEG_INF)
    sk = sink.astype(jnp.float32)[None, :, :, None, None]
    m = jnp.maximum(jnp.max(s, axis=-1, keepdims=True), sk)
    p = jnp.exp(s - m)
    denom = jnp.sum(p, axis=-1, keepdims=True) + jnp.exp(sk - m)
    p = (p / denom).astype(v.dtype)
    return jnp.einsum('bkgqj,bjkd->bqkgd', p, v)


def gqa_mixer(x, cos, sin, w_qkv, sink, w_o):
    B, L, _ = x.shape
    S = L - N_META
    nb = S // BLOCK
    qd = GQA_Q_HEADS * GQA_HEAD_DIM
    kd = GQA_KV_HEADS * GQA_HEAD_DIM
    qkv = x @ w_qkv
    q = partial_rope(qkv[..., :qd].reshape(B, L, GQA_Q_HEADS, GQA_HEAD_DIM), cos, sin)
    k = partial_rope(qkv[..., qd:qd + kd].reshape(B, L, GQA_KV_HEADS, GQA_HEAD_DIM), cos, sin)
    v = qkv[..., qd + kd:].reshape(B, L, GQA_KV_HEADS, GQA_HEAD_DIM)
    q = q.reshape(B, L, GQA_KV_HEADS, GQA_GROUP, GQA_HEAD_DIM)
    sink = sink.reshape(GQA_KV_HEADS, GQA_GROUP)
    km, vm = k[:, :N_META], v[:, :N_META]
    kr, vr = k[:, N_META:], v[:, N_META:]
    nk = N_META + WINDOW
    meta_mask = jnp.arange(nk)[None, :] <= jnp.arange(N_META)[:, None] + WINDOW
    out_meta = sink_attend(q[:, :N_META], k[:, :nk], v[:, :nk], meta_mask, sink)
    kr_pad = jnp.pad(kr, ((0, 0), (BLOCK, BLOCK), (0, 0), (0, 0)))
    vr_pad = jnp.pad(vr, ((0, 0), (BLOCK, BLOCK), (0, 0), (0, 0)))
    span = 3 * BLOCK
    qi = jnp.arange(BLOCK)[:, None]
    kj = jnp.arange(span)[None, :]
    meta_vis = jnp.ones((BLOCK, N_META), dtype=bool)

    def block_fn(args):
        b, qb = args
        kb = lax.dynamic_slice_in_dim(kr_pad, b * BLOCK, span, axis=1)
        vb = lax.dynamic_slice_in_dim(vr_pad, b * BLOCK, span, axis=1)
        rk = b * BLOCK - BLOCK + kj
        band = (jnp.abs(qi + BLOCK - kj) <= WINDOW) & (rk >= 0) & (rk < S)
        mask = jnp.concatenate([meta_vis, band], axis=1)
        return sink_attend(qb, jnp.concatenate([km, kb], axis=1), jnp.concatenate([vm, vb], axis=1), mask, sink)

    q_blocks = q[:, N_META:].reshape(B, nb, BLOCK, GQA_KV_HEADS, GQA_GROUP, GQA_HEAD_DIM).swapaxes(0, 1)
    out_real = lax.map(block_fn, (jnp.arange(nb), q_blocks)).swapaxes(0, 1).reshape(B, S, qd)
    out = jnp.concatenate([out_meta.reshape(B, N_META, qd), out_real], axis=1)
    return out @ w_o


def sqrelu_mlp(x, w1, w2):
    h = jax.nn.relu(x @ w1)
    return (h * h) @ w2


def trunk(x, meta_tokens, mla_w_in, mla_g_q, mla_w_uq, mla_g_kv, mla_w_ukv, mla_w_o,
          gqa_w_qkv, gqa_sink, gqa_w_o, mlp_w1, mlp_w2, ln1_g, ln1_b, ln2_g, ln2_b):
    B, S, D = x.shape
    meta = jnp.broadcast_to(meta_tokens[None].astype(x.dtype), (B, N_META, D))
    h = jnp.concatenate([meta, x], axis=1)
    L = S + N_META
    cos_a, sin_a = rope_tables(L, MLA_ROPE, x.dtype)
    cos_b, sin_b = rope_tables(L, GQA_ROT, x.dtype)
    for i in range(DEPTH):
        j = i // 2
        if i % 2 == 0:
            mix = mla_mixer(h, cos_a, sin_a, mla_w_in[j], mla_g_q[j], mla_w_uq[j],
                            mla_g_kv[j], mla_w_ukv[j], mla_w_o[j])
        else:
            mix = gqa_mixer(h, cos_b, sin_b, gqa_w_qkv[j], gqa_sink[j], gqa_w_o[j])
        h = layer_norm(DN_ALPHA * h + mix, ln1_g[i], ln1_b[i])
        h = layer_norm(DN_ALPHA * h + sqrelu_mlp(h, mlp_w1[i], mlp_w2[i]), ln2_g[i], ln2_b[i])
    return h[:, N_META:]


def setup_inputs(seed: int = 0) -> dict:
    key = jax.random.key(seed)
    ks = jax.random.split(key, 20)

    def nrm(k, shape, scale):
        return jax.random.normal(k, shape, jnp.float32) * scale

    d = D_MODEL
    a, g = N_MLA_LAYERS, N_GQA_LAYERS
    qkv_out = (GQA_Q_HEADS + 2 * GQA_KV_HEADS) * GQA_HEAD_DIM
    return {
        'x_prompt': nrm(ks[0], (BATCH, SEQ, d), 1.0),
        'x_sample': nrm(ks[1], (DEC_BATCH, DEC_SEQ, d), 1.0),
        'meta_tokens': nrm(ks[2], (N_META, d), 1.0),
        'mla_w_in': nrm(ks[3], (a, d, MLA_Q_RANK + MLA_KV_RANK + MLA_ROPE), d ** -0.5),
        'mla_g_q': 1.0 + nrm(ks[4], (a, MLA_Q_RANK), 0.02),
        'mla_w_uq': nrm(ks[5], (a, MLA_Q_RANK, MLA_HEADS * (MLA_NOPE + MLA_ROPE)), MLA_Q_RANK ** -0.5),
        'mla_g_kv': 1.0 + nrm(ks[6], (a, MLA_KV_RANK), 0.02),
        'mla_w_ukv': nrm(ks[7], (a, MLA_KV_RANK, MLA_HEADS * (MLA_NOPE + MLA_V)), MLA_KV_RANK ** -0.5),
        'mla_w_o': nrm(ks[8], (a, MLA_HEADS * MLA_V, d), DN_BETA * (MLA_HEADS * MLA_V) ** -0.5),
        'gqa_w_qkv': nrm(ks[9], (g, d, qkv_out), d ** -0.5),
        'gqa_sink': nrm(ks[10], (g, GQA_Q_HEADS), 0.5),
        'gqa_w_o': nrm(ks[11], (g, GQA_Q_HEADS * GQA_HEAD_DIM, d), DN_BETA * (GQA_Q_HEADS * GQA_HEAD_DIM) ** -0.5),
        'mlp_w1': nrm(ks[12], (DEPTH, d, D_FF), d ** -0.5),
        'mlp_w2': nrm(ks[13], (DEPTH, D_FF, d), DN_BETA * D_FF ** -0.5),
        'ln1_g': 1.0 + nrm(ks[14], (DEPTH, d), 0.02),
        'ln1_b': nrm(ks[15], (DEPTH, d), 0.02),
        'ln2_g': 1.0 + nrm(ks[16], (DEPTH, d), 0.02),
        'ln2_b': nrm(ks[17], (DEPTH, d), 0.02),
    }


def reference(x_prompt, x_sample, meta_tokens, mla_w_in, mla_g_q, mla_w_uq, mla_g_kv, mla_w_ukv, mla_w_o,
              gqa_w_qkv, gqa_sink, gqa_w_o, mlp_w1, mlp_w2, ln1_g, ln1_b, ln2_g, ln2_b):
    y_prompt = trunk(x_prompt, meta_tokens, mla_w_in, mla_g_q, mla_w_uq, mla_g_kv, mla_w_ukv, mla_w_o,
                     gqa_w_qkv, gqa_sink, gqa_w_o, mlp_w1, mlp_w2, ln1_g, ln1_b, ln2_g, ln2_b)
    y_sample = trunk(x_sample, meta_tokens, mla_w_in, mla_g_q, mla_w_uq, mla_g_kv, mla_w_ukv, mla_w_o,
                     gqa_w_qkv, gqa_sink, gqa_w_o, mlp_w1, mlp_w2, ln1_g, ln1_b, ln2_g, ln2_b)
    return (y_prompt, y_sample)
```

```python
import functools
import math

import jax
import jax.numpy as jnp
from jax import lax
from jax.experimental import pallas as pl
from jax.experimental.pallas import tpu as pltpu

D_MODEL = 1024
N_META = 16
BLOCK = 128
WINDOW = 128
ROPE_THETA = 500000.0
MLA_HEADS = 16
MLA_NOPE = 64
MLA_ROPE = 32
MLA_V = 64
MLA_KV_RANK = 256
MLA_Q_RANK = 768
GQA_Q_HEADS = 16
GQA_KV_HEADS = 4
GQA_HEAD_DIM = 64
GQA_ROT = 16
D_FF = 4096
LN_EPS = 1e-5
RMS_EPS = 1e-6
NEG_INF = -1e30
LOG2E = math.log2(math.e)

LANES = 128
VMEM_LIMIT_BYTES = 56 * 1024 * 1024

BF16 = jnp.bfloat16
F32 = jnp.float32


def _row_tile(rows, target):
    if rows <= target:
        return rows
    t = target
    while rows % t:
        t //= 2
    return t


def _const_spec(shape):
    nd = len(shape)
    return pl.BlockSpec(shape, lambda *_: (0,) * nd, pipeline_mode=pl.Buffered(1))


def _nt_dot(a, b):
    return lax.dot_general(a, b, (((1,), (1,)), ((), ())), preferred_element_type=F32)


def _dot(a, b):
    return jnp.dot(a, b, preferred_element_type=F32)


def _mla_proj_kernel(x_ref, tq_ref, ta_ref, tb_ref, w_in_ref, gq_ref, gkv_ref, wq_ref, wk_ref, wv_ref,
                     q_ref, k_ref, v_ref):
    xb = x_ref[...].astype(BF16)
    c = _dot(xb, w_in_ref[...])
    cq = c[:, :MLA_Q_RANK]
    cq = cq * lax.rsqrt(jnp.mean(cq * cq, axis=-1, keepdims=True) + RMS_EPS) * gq_ref[...]
    kv0 = MLA_Q_RANK
    ckv = c[:, kv0:kv0 + MLA_KV_RANK]
    ckv = ckv * lax.rsqrt(jnp.mean(ckv * ckv, axis=-1, keepdims=True) + RMS_EPS) * gkv_ref[...]
    r0 = kv0 + MLA_KV_RANK
    kr = c[:, r0:r0 + LANES] * ta_ref[...] + c[:, r0 + LANES:r0 + 2 * LANES] * tb_ref[...]
    cqb = cq.astype(BF16)
    ckvb = ckv.astype(BF16)
    q = _dot(cqb, wq_ref[...])
    k = _dot(ckvb, wk_ref[...])
    tq = tq_ref[...]
    for h in range(MLA_HEADS):
        sl = slice(h * LANES, (h + 1) * LANES)
        q_ref[:, sl] = (q[:, sl] * tq).astype(BF16)
        k_ref[:, sl] = (k[:, sl] + kr).astype(BF16)
    v_ref[...] = _dot(ckvb, wv_ref[...]).astype(BF16)


def _mla_proj(x2d, tabs, w, tm):
    rows = x2d.shape[0]
    nper = tabs['tq'].shape[0] // tm
    row_spec = lambda n: pl.BlockSpec((tm, n), lambda i: (i, 0))
    tab_spec = pl.BlockSpec((tm, LANES), lambda i: (i % nper, 0))
    hq = MLA_HEADS * LANES
    return pl.pallas_call(
        _mla_proj_kernel,
        grid=(rows // tm,),
        in_specs=[row_spec(D_MODEL), tab_spec, tab_spec, tab_spec,
                  _const_spec(w['w_in'].shape), _const_spec(w['g_q'].shape), _const_spec(w['g_kv'].shape),
                  _const_spec(w['wq'].shape), _const_spec(w['wk'].shape), _const_spec(w['wv'].shape)],
        out_specs=[row_spec(hq), row_spec(hq), row_spec(MLA_HEADS * MLA_V)],
        out_shape=[jax.ShapeDtypeStruct((rows, hq), BF16), jax.ShapeDtypeStruct((rows, hq), BF16),
                   jax.ShapeDtypeStruct((rows, MLA_HEADS * MLA_V), BF16)],
        compiler_params=pltpu.CompilerParams(dimension_semantics=("parallel",),
                                             vmem_limit_bytes=VMEM_LIMIT_BYTES),
        name="mla_proj",
    )(x2d, tabs['tq'], tabs['ta'], tabs['tb'], w['w_in'], w['g_q'], w['g_kv'], w['wq'], w['wk'], w['wv'])


def _mla_attn_kernel(qr_ref, kr_ref, vr_ref, qm_ref, km_ref, vm_ref, or_ref, om_ref,
                     vxr_ref, vxm_ref, kms_ref, *, tq):
    seq = qr_ref.shape[1]
    vxr_ref[:, :LANES] = vr_ref[0]
    vxr_ref[:, LANES:] = jnp.ones((seq, LANES), BF16)
    vxm_ref[...] = jnp.zeros(vxm_ref.shape, BF16)
    vxm_ref[:N_META, :LANES] = vm_ref[0]
    vxm_ref[:N_META, LANES:] = jnp.ones((N_META, LANES), BF16)
    kms_ref[...] = jnp.zeros(kms_ref.shape, BF16)
    for t in range(2):
        kms_ref[t, :N_META, :] = km_ref[0, :, t * LANES:(t + 1) * LANES]
    lane = lax.broadcasted_iota(jnp.int32, (1, LANES), 1)
    meta_valid = lane < N_META
    first_head = lane < MLA_V

    def attend(q, t):
        s_r = _nt_dot(q, kr_ref[0, :, t * LANES:(t + 1) * LANES])
        s_m = jnp.where(meta_valid, _nt_dot(q, kms_ref[t]), NEG_INF)
        m = jnp.maximum(jnp.max(s_r, axis=-1, keepdims=True), jnp.max(s_m, axis=-1, keepdims=True))
        p_r = jnp.exp2(s_r - m).astype(BF16)
        p_m = jnp.exp2(s_m - m).astype(BF16)
        o = _dot(p_r, vxr_ref[...]) + _dot(p_m, vxm_ref[...])
        return o[:, :LANES] / o[:, LANES:]

    def both_heads(q_pair):
        res = [attend(q_pair[:, t * LANES:(t + 1) * LANES], t) for t in range(2)]
        return jnp.where(first_head, res[0], res[1]).astype(BF16)

    def chunk(ci, carry):
        r0 = pl.multiple_of(ci * tq, tq)
        or_ref[0, pl.ds(r0, tq), :] = both_heads(qr_ref[0, pl.ds(r0, tq), :])
        return carry

    lax.fori_loop(0, seq // tq, chunk, 0)
    om_ref[0] = both_heads(qm_ref[0])


def _mla_attn(q_r, k_r, v_r, q_m, k_m, v_m):
    b, seq, _ = q_r.shape
    tq = _row_tile(seq, 512 if seq <= 2048 else 256)
    pair = 2 * LANES
    real = lambda n: pl.BlockSpec((1, seq, n), lambda i, j: (i, 0, j))
    meta = lambda n: pl.BlockSpec((1, N_META, n), lambda i, j: (i, 0, j))
    return pl.pallas_call(
        functools.partial(_mla_attn_kernel, tq=tq),
        grid=(b, MLA_HEADS // 2),
        in_specs=[real(pair), real(pair), real(LANES), meta(pair), meta(pair), meta(LANES)],
        out_specs=[real(LANES), meta(LANES)],
        out_shape=[jax.ShapeDtypeStruct((b, seq, MLA_HEADS * MLA_V), BF16),
                   jax.ShapeDtypeStruct((b, N_META, MLA_HEADS * MLA_V), BF16)],
        scratch_shapes=[pltpu.VMEM((seq, pair), BF16), pltpu.VMEM((LANES, pair), BF16),
                        pltpu.VMEM((2, LANES, LANES), BF16)],
        compiler_params=pltpu.CompilerParams(dimension_semantics=("parallel", "parallel"),
                                             vmem_limit_bytes=VMEM_LIMIT_BYTES),
        name="mla_attn",
    )(q_r, k_r, v_r, q_m, k_m, v_m)


def _layer_norm(x, g, b):
    mu = jnp.mean(x, axis=-1, keepdims=True)
    xc = x - mu
    var = jnp.mean(xc * xc, axis=-1, keepdims=True)
    return xc * lax.rsqrt(var + LN_EPS) * g + b


def _post_kernel(a_ref, h_ref, wo_ref, g1_ref, b1_ref, w1_ref, w2_ref, g2_ref, b2_ref, o_ref, *, alpha, ff_chunk):
    mix = _dot(a_ref[...], wo_ref[...])
    h1 = _layer_norm(alpha * h_ref[...] + mix, g1_ref[...], b1_ref[...])
    h1b = h1.astype(BF16)
    acc = jnp.zeros(h1.shape, F32)
    for c in range(D_FF // ff_chunk):
        sl = slice(c * ff_chunk, (c + 1) * ff_chunk)
        u = jnp.maximum(_dot(h1b, w1_ref[:, sl]), 0.0)
        acc = acc + _dot((u * u).astype(BF16), w2_ref[sl, :])
    o_ref[...] = _layer_norm(alpha * h1 + acc, g2_ref[...], b2_ref[...])


def _post(a2d, h2d, w, alpha, tm):
    rows = a2d.shape[0]
    row_spec = pl.BlockSpec((tm, D_MODEL), lambda i: (i, 0))
    names = ('wo', 'g1', 'b1', 'w1', 'w2', 'g2', 'b2')
    return pl.pallas_call(
        functools.partial(_post_kernel, alpha=alpha, ff_chunk=1024),
        grid=(rows // tm,),
        in_specs=[row_spec, row_spec] + [_const_spec(w[n].shape) for n in names],
        out_specs=row_spec,
        out_shape=jax.ShapeDtypeStruct((rows, D_MODEL), F32),
        compiler_params=pltpu.CompilerParams(dimension_semantics=("parallel",),
                                             vmem_limit_bytes=VMEM_LIMIT_BYTES),
        name="post_mlp",
    )(a2d, h2d, *[w[n] for n in names])


def _rope_blocks(x, c, s1, s2, out_ref):
    for j in range(x.shape[1] // LANES):
        blk = x[:, j * LANES:(j + 1) * LANES]
        rot = blk * c + pltpu.roll(blk, LANES - GQA_ROT // 2, 1) * s1 + pltpu.roll(blk, GQA_ROT // 2, 1) * s2
        out_ref[:, j * LANES:(j + 1) * LANES] = rot.astype(BF16)


def _gqa_proj_kernel(*refs, with_q):
    if with_q:
        h_ref, kc_ref, ks1_ref, ks2_ref, wk_ref, wv_ref, qc_ref, qs1_ref, qs2_ref, wq_ref, k_ref, v_ref, q_ref = refs
    else:
        h_ref, kc_ref, ks1_ref, ks2_ref, wk_ref, wv_ref, k_ref, v_ref = refs
    hb = h_ref[...].astype(BF16)
    if with_q:
        _rope_blocks(_dot(hb, wq_ref[...]), qc_ref[...], qs1_ref[...], qs2_ref[...], q_ref)
    _rope_blocks(_dot(hb, wk_ref[...]), kc_ref[...], ks1_ref[...], ks2_ref[...], k_ref)
    v = _dot(hb, wv_ref[...]).astype(BF16)
    ones = jnp.ones((v.shape[0], LANES), BF16)
    for j in range(2 * GQA_KV_HEADS):
        v_ref[:, 2 * j * LANES:(2 * j + 1) * LANES] = v[:, j * LANES:(j + 1) * LANES]
        v_ref[:, (2 * j + 1) * LANES:(2 * j + 2) * LANES] = ones


def _gqa_proj(h2d, tabs, w, tm, with_q):
    rows = h2d.shape[0]
    nper = tabs['kc'].shape[0] // tm
    row_spec = lambda n: pl.BlockSpec((tm, n), lambda i: (i, 0))
    tab_spec = pl.BlockSpec((tm, LANES), lambda i: (i % nper, 0))
    nq = GQA_Q_HEADS * GQA_HEAD_DIM
    nk = GQA_KV_HEADS * 2 * LANES
    nv = GQA_KV_HEADS * 4 * LANES
    args = [h2d, tabs['kc'], tabs['ks1'], tabs['ks2'], w['wk'], w['wv']]
    in_specs = [row_spec(D_MODEL)] + [tab_spec] * 3 + [_const_spec(w['wk'].shape), _const_spec(w['wv'].shape)]
    out_specs = [row_spec(nk), row_spec(nv)]
    out_shape = [jax.ShapeDtypeStruct((rows, nk), BF16), jax.ShapeDtypeStruct((rows, nv), BF16)]
    if with_q:
        args += [tabs['qc'], tabs['qs1'], tabs['qs2'], w['wq']]
        in_specs += [tab_spec] * 3 + [_const_spec(w['wq'].shape)]
        out_specs.append(row_spec(nq))
        out_shape.append(jax.ShapeDtypeStruct((rows, nq), BF16))
    return pl.pallas_call(
        functools.partial(_gqa_proj_kernel, with_q=with_q),
        grid=(rows // tm,),
        in_specs=in_specs,
        out_specs=out_specs,
        out_shape=out_shape,
        compiler_params=pltpu.CompilerParams(dimension_semantics=("parallel",),
                                             vmem_limit_bytes=VMEM_LIMIT_BYTES),
        name="gqa_proj",
    )(*args)


def _gqa_attn_kernel(sink_ref, q_ref, kp_ref, kc_ref, kn_ref, vp_ref, vc_ref, vn_ref, km_ref, vm_ref, o_ref,
                     kw_ref, vw_ref, *, tq):
    i = pl.program_id(1)
    nsub = tq // BLOCK
    nblk = pl.num_programs(1) * nsub
    kw_ref[0:BLOCK, :] = jnp.zeros((BLOCK, kw_ref.shape[1]), BF16)
    vw_ref[0:BLOCK, :] = jnp.zeros((BLOCK, vw_ref.shape[1]), BF16)
    kw_ref[0:N_META, :] = km_ref[0]
    vw_ref[0:N_META, :] = vm_ref[0]
    kw_ref[BLOCK:2 * BLOCK, :] = kp_ref[0]
    vw_ref[BLOCK:2 * BLOCK, :] = vp_ref[0]
    kw_ref[2 * BLOCK:2 * BLOCK + tq, :] = kc_ref[0]
    vw_ref[2 * BLOCK:2 * BLOCK + tq, :] = vc_ref[0]
    kw_ref[2 * BLOCK + tq:3 * BLOCK + tq, :] = kn_ref[0]
    vw_ref[2 * BLOCK + tq:3 * BLOCK + tq, :] = vn_ref[0]

    span = 3 * BLOCK
    rows2 = 2 * BLOCK
    qi = lax.broadcasted_iota(jnp.int32, (rows2, span), 0) % BLOCK
    kj = lax.broadcasted_iota(jnp.int32, (rows2, span), 1)
    band = (kj >= qi) & (kj <= qi + 2 * WINDOW)
    lane = lax.broadcasted_iota(jnp.int32, (1, LANES), 1)
    meta_valid = lane < N_META
    first_head = lane < GQA_HEAD_DIM
    upper_rows = lax.broadcasted_iota(jnp.int32, (rows2, 1), 0) >= BLOCK

    def sub_block(s, carry):
        q0 = pl.multiple_of(s * BLOCK, BLOCK)
        w0 = pl.multiple_of((s + 1) * BLOCK, BLOCK)
        blk = i * nsub + s
        mask = band & ((kj >= BLOCK) | (blk > 0)) & ((kj < 2 * BLOCK) | (blk < nblk - 1))
        for g in range(GQA_KV_HEADS):
            qa = q_ref[0, pl.ds(q0, BLOCK), (2 * g) * LANES:(2 * g + 1) * LANES]
            qb = q_ref[0, pl.ds(q0, BLOCK), (2 * g + 1) * LANES:(2 * g + 2) * LANES]
            lhs = jnp.concatenate([qa, qb], axis=0)
            outs, invs = [], []
            for par in range(2):
                kl = (2 * g + par) * LANES
                vl = (2 * g + par) * 2 * LANES
                s_w = jnp.where(mask, _nt_dot(lhs, kw_ref[pl.ds(w0, span), kl:kl + LANES]), NEG_INF)
                s_m = jnp.where(meta_valid, _nt_dot(lhs, kw_ref[0:BLOCK, kl:kl + LANES]), NEG_INF)
                sink = jnp.where(upper_rows, sink_ref[4 * g + 2 + par], sink_ref[4 * g + par])
                m = jnp.maximum(jnp.maximum(jnp.max(s_w, axis=-1, keepdims=True),
                                            jnp.max(s_m, axis=-1, keepdims=True)), sink)
                p_w = jnp.exp2(s_w - m).astype(BF16)
                p_m = jnp.exp2(s_m - m).astype(BF16)
                o = (_dot(p_w, vw_ref[pl.ds(w0, span), vl:vl + 2 * LANES]) +
                     _dot(p_m, vw_ref[0:BLOCK, vl:vl + 2 * LANES]))
                outs.append(o[:, :LANES])
                invs.append(1.0 / (o[:, LANES:] + jnp.exp2(sink - m)))
            res = ((outs[0] + outs[1]) * jnp.where(first_head, invs[0], invs[1])).astype(BF16)
            o_ref[0, pl.ds(q0, BLOCK), (2 * g) * LANES:(2 * g + 1) * LANES] = res[:BLOCK]
            o_ref[0, pl.ds(q0, BLOCK), (2 * g + 1) * LANES:(2 * g + 2) * LANES] = res[BLOCK:]
        return carry

    lax.fori_loop(0, nsub, sub_block, 0)


def _gqa_attn(sink2, q, k, v, k_m, v_m):
    b, seq, nq = q.shape
    nk, nv = k.shape[2], v.shape[2]
    tq = _row_tile(seq, 512)
    nsub = tq // BLOCK
    nblk = seq // BLOCK
    tile = lambda n: pl.BlockSpec((1, tq, n), lambda bi, i: (bi, i, 0))
    prev = lambda n: pl.BlockSpec((1, BLOCK, n), lambda bi, i: (bi, jnp.maximum(i * nsub - 1, 0), 0))
    nxt = lambda n: pl.BlockSpec((1, BLOCK, n), lambda bi, i: (bi, jnp.minimum((i + 1) * nsub, nblk - 1), 0))
    meta = lambda n: pl.BlockSpec((1, N_META, n), lambda bi, i: (bi, 0, 0))
    return pl.pallas_call(
        functools.partial(_gqa_attn_kernel, tq=tq),
        grid=(b, seq // tq),
        in_specs=[pl.BlockSpec(memory_space=pltpu.SMEM), tile(nq), prev(nk), tile(nk), nxt(nk),
                  prev(nv), tile(nv), nxt(nv), meta(nk), meta(nv)],
        out_specs=tile(nq),
        out_shape=jax.ShapeDtypeStruct((b, seq, nq), BF16),
        scratch_shapes=[pltpu.VMEM((tq + 3 * BLOCK, nk), BF16), pltpu.VMEM((tq + 3 * BLOCK, nv), BF16)],
        compiler_params=pltpu.CompilerParams(dimension_semantics=("parallel", "parallel"),
                                             vmem_limit_bytes=VMEM_LIMIT_BYTES),
        name="gqa_attn",
    )(sink2, q, k, k, k, v, v, v, k_m, v_m)


def _swap_halves(w):
    half = w.shape[-1] // 2
    return jnp.concatenate([-w[..., half:], w[..., :half]], axis=-1)


def _prep_mla(w_in, g_q, w_uq, g_kv, w_ukv, w_o):
    d = w_in.shape[0]
    z64 = jnp.zeros((d, MLA_NOPE), F32)
    wkr = w_in[:, MLA_Q_RANK + MLA_KV_RANK:]
    wkr_sw = _swap_halves(wkr)
    w_in_x = jnp.concatenate([w_in[:, :MLA_Q_RANK + MLA_KV_RANK], z64, wkr, wkr, z64, wkr_sw, wkr_sw], axis=1)
    wq = w_uq.reshape(MLA_Q_RANK, MLA_HEADS, MLA_NOPE + MLA_ROPE)
    wq = jnp.concatenate([wq, _swap_halves(wq[..., MLA_NOPE:])], axis=-1).reshape(MLA_Q_RANK, MLA_HEADS * LANES)
    wkv = w_ukv.reshape(MLA_KV_RANK, MLA_HEADS, MLA_NOPE + MLA_V)
    wk = jnp.concatenate([wkv[..., :MLA_NOPE], jnp.zeros_like(wkv[..., :MLA_NOPE])], axis=-1)
    wk = wk.reshape(MLA_KV_RANK, MLA_HEADS * LANES)
    wv = wkv[..., MLA_NOPE:].reshape(MLA_KV_RANK, MLA_HEADS * MLA_V)
    return dict(w_in=w_in_x.astype(BF16), g_q=g_q.reshape(1, -1), g_kv=g_kv.reshape(1, -1),
                wq=wq.astype(BF16), wk=wk.astype(BF16), wv=wv.astype(BF16)), w_o.astype(BF16)


def _prep_gqa(w_qkv, w_o):
    d = w_qkv.shape[0]
    qd = GQA_Q_HEADS * GQA_HEAD_DIM
    kd = GQA_KV_HEADS * GQA_HEAD_DIM
    z = jnp.zeros((d, GQA_KV_HEADS, GQA_HEAD_DIM), F32)

    def lo_hi(w):
        w = w.reshape(d, GQA_KV_HEADS, GQA_HEAD_DIM)
        return jnp.concatenate([w, z, z, w], axis=-1).reshape(d, GQA_KV_HEADS * 2 * LANES)

    return dict(wq=w_qkv[:, :qd].astype(BF16), wk=lo_hi(w_qkv[:, qd:qd + kd]).astype(BF16),
                wv=lo_hi(w_qkv[:, qd + kd:]).astype(BF16)), w_o.astype(BF16)


def _rope_angles(length, dim):
    pos = jnp.arange(length, dtype=F32)
    inv = ROPE_THETA ** (-jnp.arange(0, dim, 2, dtype=F32) / dim)
    ang = pos[:, None] * inv[None, :]
    return jnp.cos(ang), jnp.sin(ang)


def _mla_tables(length):
    cos, sin = _rope_angles(length, MLA_ROPE)
    c = jnp.concatenate([cos, cos], axis=1)
    s = jnp.concatenate([sin, sin], axis=1)
    one = jnp.ones((length, MLA_NOPE), F32)
    zero = jnp.zeros((length, MLA_NOPE), F32)
    scale = (MLA_NOPE + MLA_ROPE) ** -0.5 * LOG2E
    return dict(tq=jnp.concatenate([one, c, s], axis=1) * scale,
                ta=jnp.concatenate([zero, c, c], axis=1), tb=jnp.concatenate([zero, s, s], axis=1))


def _gqa_tables(length):
    cos, sin = _rope_angles(length, GQA_ROT)
    rest = GQA_HEAD_DIM - GQA_ROT
    zr = jnp.zeros((length, rest), F32)
    z8 = jnp.zeros_like(sin)
    c = jnp.concatenate([cos, cos, jnp.ones((length, rest), F32)], axis=1)
    s1 = jnp.concatenate([-sin, z8, zr], axis=1)
    s2 = jnp.concatenate([z8, sin, zr], axis=1)
    two = lambda t: jnp.concatenate([t, t], axis=1)
    scale = GQA_HEAD_DIM ** -0.5 * LOG2E
    return dict(kc=two(c), ks1=two(s1), ks2=two(s2), qc=two(c) * scale, qs1=two(s1) * scale, qs2=two(s2) * scale)


def _split_tables(tabs, batch):
    real = {n: t[N_META:] for n, t in tabs.items()}
    meta = {n: jnp.tile(t[:N_META], (batch, 1)) for n, t in tabs.items()}
    return real, meta


def _trunk(x, meta_tokens, layers, tile_rows=512):
    b, seq, d = x.shape
    depth = len(layers)
    alpha = (2.0 * depth) ** 0.25
    length = seq + N_META
    h_r = x.reshape(b * seq, d)
    h_m = jnp.broadcast_to(meta_tokens[None].astype(x.dtype), (b, N_META, d)).reshape(b * N_META, d)
    mla_r, mla_m = _split_tables(_mla_tables(length), b)
    gqa_r, gqa_m = _split_tables(_gqa_tables(length), b)
    tm_r = _row_tile(seq, tile_rows)
    tm_m = _row_tile(b * N_META, tile_rows)
    shp = lambda t, n: t.reshape(b, n, t.shape[-1])
    for i, layer in enumerate(layers):
        last = i == depth - 1
        if layer['kind'] == 'mla':
            q_r, k_r, v_r = _mla_proj(h_r, mla_r, layer['proj'], tm_r)
            q_m, k_m, v_m = _mla_proj(h_m, mla_m, layer['proj'], tm_m)
            a_r, a_m = _mla_attn(shp(q_r, seq), shp(k_r, seq), shp(v_r, seq),
                                 shp(q_m, N_META), shp(k_m, N_META), shp(v_m, N_META))
        else:
            assert last, "windowed layer computes real-token outputs only"
            k_r, v_r, q_r = _gqa_proj(h_r, gqa_r, layer['proj'], tm_r, True)
            k_m, v_m = _gqa_proj(h_m, gqa_m, layer['proj'], tm_m, False)
            a_r = _gqa_attn(layer['sink2'], shp(q_r, seq), shp(k_r, seq), shp(v_r, seq),
                            shp(k_m, N_META), shp(v_m, N_META))
            a_m = None
        h_r = _post(a_r.reshape(b * seq, d), h_r, layer['post'], alpha, tm_r)
        if not last:
            h_m = _post(a_m.reshape(b * N_META, d), h_m, layer['post'], alpha, tm_m)
    return h_r.reshape(b, seq, d)


def kernel(x_prompt, x_sample, meta_tokens, mla_w_in, mla_g_q, mla_w_uq, mla_g_kv, mla_w_ukv, mla_w_o,
           gqa_w_qkv, gqa_sink, gqa_w_o, mlp_w1, mlp_w2, ln1_g, ln1_b, ln2_g, ln2_b):
    depth = mlp_w1.shape[0]
    layers = []
    for i in range(depth):
        j = i // 2
        if i % 2 == 0:
            proj, wo = _prep_mla(mla_w_in[j], mla_g_q[j], mla_w_uq[j], mla_g_kv[j], mla_w_ukv[j], mla_w_o[j])
            layer = dict(kind='mla', proj=proj)
        else:
            proj, wo = _prep_gqa(gqa_w_qkv[j], gqa_w_o[j])
            layer = dict(kind='gqa', proj=proj, sink2=gqa_sink[j].astype(F32) * LOG2E)
        row = lambda t: t[i].reshape(1, -1).astype(F32)
        layer['post'] = dict(wo=wo, g1=row(ln1_g), b1=row(ln1_b), w1=mlp_w1[i].astype(BF16),
                             w2=mlp_w2[i].astype(BF16), g2=row(ln2_g), b2=row(ln2_b))
        layers.append(layer)
    return (_trunk(x_prompt, meta_tokens, layers), _trunk(x_sample, meta_tokens, layers))
```

```python
import functools
import math

import jax
import jax.numpy as jnp
from jax import lax
from jax.experimental import pallas as pl
from jax.experimental.pallas import tpu as pltpu

D_MODEL = 1024
N_META = 16
BLOCK = 128
WINDOW = 128
ROPE_THETA = 500000.0
MLA_HEADS = 16
MLA_NOPE = 64
MLA_ROPE = 32
MLA_V = 64
MLA_KV_RANK = 256
MLA_Q_RANK = 768
GQA_Q_HEADS = 16
GQA_KV_HEADS = 4
GQA_HEAD_DIM = 64
GQA_ROT = 16
D_FF = 4096
LN_EPS = 1e-5
RMS_EPS = 1e-6
NEG_INF = -1e30
LOG2E = math.log2(math.e)

LANES = 128
VMEM_LIMIT_BYTES = 56 * 1024 * 1024

BF16 = jnp.bfloat16
F32 = jnp.float32


def _row_tile(rows, target):
    if rows <= target:
        return rows
    t = target
    while rows % t:
        t //= 2
    return t


def _const_spec(shape):
    nd = len(shape)
    return pl.BlockSpec(shape, lambda *_: (0,) * nd, pipeline_mode=pl.Buffered(1))


def _nt_dot(a, b):
    return lax.dot_general(a, b, (((1,), (1,)), ((), ())), preferred_element_type=F32)


def _dot(a, b):
    return jnp.dot(a, b, preferred_element_type=F32)


def _mla_proj_kernel(x_ref, tq_ref, ta_ref, tb_ref, w_in_ref, gq_ref, gkv_ref, wq_ref, wk_ref, wv_ref,
                     q_ref, k_ref, v_ref):
    xb = x_ref[...].astype(BF16)
    c = _dot(xb, w_in_ref[...])
    cq = c[:, :MLA_Q_RANK]
    cq = cq * lax.rsqrt(jnp.mean(cq * cq, axis=-1, keepdims=True) + RMS_EPS) * gq_ref[...]
    kv0 = MLA_Q_RANK
    ckv = c[:, kv0:kv0 + MLA_KV_RANK]
    ckv = ckv * lax.rsqrt(jnp.mean(ckv * ckv, axis=-1, keepdims=True) + RMS_EPS) * gkv_ref[...]
    r0 = kv0 + MLA_KV_RANK
    kr = c[:, r0:r0 + LANES] * ta_ref[...] + c[:, r0 + LANES:r0 + 2 * LANES] * tb_ref[...]
    cqb = cq.astype(BF16)
    ckvb = ckv.astype(BF16)
    q = _dot(cqb, wq_ref[...])
    k = _dot(ckvb, wk_ref[...])
    tq = tq_ref[...]
    for h in range(MLA_HEADS):
        sl = slice(h * LANES, (h + 1) * LANES)
        q_ref[:, sl] = (q[:, sl] * tq).astype(BF16)
        k_ref[:, sl] = (k[:, sl] + kr).astype(BF16)
    v_ref[...] = _dot(ckvb, wv_ref[...]).astype(BF16)


def _mla_proj(x2d, tabs, w, tm):
    rows = x2d.shape[0]
    nper = tabs['tq'].shape[0] // tm
    row_spec = lambda n: pl.BlockSpec((tm, n), lambda i: (i, 0))
    tab_spec = pl.BlockSpec((tm, LANES), lambda i: (i % nper, 0))
    hq = MLA_HEADS * LANES
    return pl.pallas_call(
        _mla_proj_kernel,
        grid=(rows // tm,),
        in_specs=[row_spec(D_MODEL), tab_spec, tab_spec, tab_spec,
                  _const_spec(w['w_in'].shape), _const_spec(w['g_q'].shape), _const_spec(w['g_kv'].shape),
                  _const_spec(w['wq'].shape), _const_spec(w['wk'].shape), _const_spec(w['wv'].shape)],
        out_specs=[row_spec(hq), row_spec(hq), row_spec(MLA_HEADS * MLA_V)],
        out_shape=[jax.ShapeDtypeStruct((rows, hq), BF16), jax.ShapeDtypeStruct((rows, hq), BF16),
                   jax.ShapeDtypeStruct((rows, MLA_HEADS * MLA_V), BF16)],
        compiler_params=pltpu.CompilerParams(dimension_semantics=("parallel",),
                                             vmem_limit_bytes=VMEM_LIMIT_BYTES),
        name="mla_proj",
    )(x2d, tabs['tq'], tabs['ta'], tabs['tb'], w['w_in'], w['g_q'], w['g_kv'], w['wq'], w['wk'], w['wv'])


MLA_Q_ROWS = 512
MLA_KEY_BLOCK = 2048


def _mla_attn_kernel(qr_ref, kr_ref, vr_ref, qm_ref, km_ref, vm_ref, or_ref, om_ref,
                     vxr_ref, vxm_ref, kms_ref, qmx_ref, acc_ref, m_ref, *, tq, kb):
    seq = qr_ref.shape[1]
    pair = 2 * LANES
    nkb = seq // kb
    vxr_ref[:, :LANES] = vr_ref[0]
    vxr_ref[:, LANES:] = jnp.ones((seq, LANES), BF16)
    vxm_ref[...] = jnp.zeros(vxm_ref.shape, BF16)
    vxm_ref[:N_META, :LANES] = vm_ref[0]
    vxm_ref[:N_META, LANES:] = jnp.ones((N_META, LANES), BF16)
    kms_ref[...] = jnp.zeros(kms_ref.shape, BF16)
    kms_ref[:N_META, :] = km_ref[0]
    qmx_ref[...] = jnp.zeros(qmx_ref.shape, BF16)
    for t in range(2):
        qmx_ref[t * N_META:(t + 1) * N_META, t * LANES:(t + 1) * LANES] = qm_ref[0, :, t * LANES:(t + 1) * LANES]
    lane = lax.broadcasted_iota(jnp.int32, (1, LANES), 1)
    meta_valid = lane < N_META
    first_head = lane < MLA_V

    def attend(q, feat):
        parts = []
        for j in range(nkb):
            rows = slice(j * kb, (j + 1) * kb)
            s = _nt_dot(q, kr_ref[0, rows, feat])
            m = jnp.max(s, axis=-1, keepdims=True)
            if j == nkb - 1:
                s_m = jnp.where(meta_valid, _nt_dot(q, kms_ref[:, feat]), NEG_INF)
                m = jnp.maximum(m, jnp.max(s_m, axis=-1, keepdims=True))
            o = _dot(jnp.exp2(s - m).astype(BF16), vxr_ref[rows, :])
            if j == nkb - 1:
                o = o + _dot(jnp.exp2(s_m - m).astype(BF16), vxm_ref[...])
            parts.append((m, o))
        m_all, o = parts[0]
        if nkb > 1:
            for m_j, _ in parts[1:]:
                m_all = jnp.maximum(m_all, m_j)
            o = sum(o_j * jnp.exp2(m_j - m_all) for m_j, o_j in parts)
        return o[:, :LANES] / o[:, LANES:]

    def chunk(ci, carry):
        r0 = pl.multiple_of(ci * tq, tq)
        res = [attend(qr_ref[0, pl.ds(r0, tq), t * LANES:(t + 1) * LANES], slice(t * LANES, (t + 1) * LANES))
               for t in range(2)]
        or_ref[0, pl.ds(r0, tq), :] = jnp.where(first_head, res[0], res[1]).astype(BF16)
        return carry

    def chunk_block(it, carry):
        ci, j = it // nkb, it % nkb
        r0 = pl.multiple_of(ci * tq, tq)
        k0 = pl.multiple_of(j * kb, kb)
        first_block = j == 0
        with_meta = jnp.where(j == nkb - 1, 1.0, 0.0)
        res = []
        for t in range(2):
            feat = slice(t * LANES, (t + 1) * LANES)
            q = qr_ref[0, pl.ds(r0, tq), feat]
            s = _nt_dot(q, kr_ref[0, pl.ds(k0, kb), feat])
            s_m = jnp.where(meta_valid, _nt_dot(q, kms_ref[:, feat]), NEG_INF)
            m_prev = jnp.where(first_block, NEG_INF, m_ref[t])
            m = jnp.maximum(jnp.maximum(m_prev, jnp.max(s, axis=-1, keepdims=True)),
                            jnp.max(s_m, axis=-1, keepdims=True))
            o = (acc_ref[t] * jnp.exp2(m_prev - m) + _dot(jnp.exp2(s - m).astype(BF16), vxr_ref[pl.ds(k0, kb), :]) +
                 _dot((jnp.exp2(s_m - m) * with_meta).astype(BF16), vxm_ref[...]))
            acc_ref[t] = o
            m_ref[t] = m
            res.append(o[:, :LANES] / o[:, LANES:])
        or_ref[0, pl.ds(r0, tq), :] = jnp.where(first_head, res[0], res[1]).astype(BF16)
        return carry

    if nkb == 1:
        lax.fori_loop(0, seq // tq, chunk, 0)
    else:
        acc_ref[...] = jnp.zeros(acc_ref.shape, F32)
        lax.fori_loop(0, (seq // tq) * nkb, chunk_block, 0)
    res = attend(qmx_ref[...], slice(0, pair))
    om_ref[0] = jnp.where(first_head, res[:N_META], res[N_META:]).astype(BF16)


def _mla_attn(q_r, k_r, v_r, q_m, k_m, v_m):
    b, seq, _ = q_r.shape
    tq = _row_tile(seq, MLA_Q_ROWS)
    kb = _row_tile(seq, MLA_KEY_BLOCK)
    pair = 2 * LANES
    real = lambda n: pl.BlockSpec((1, seq, n), lambda i, j: (i, 0, j))
    meta = lambda n: pl.BlockSpec((1, N_META, n), lambda i, j: (i, 0, j))
    return pl.pallas_call(
        functools.partial(_mla_attn_kernel, tq=tq, kb=kb),
        grid=(b, MLA_HEADS // 2),
        in_specs=[real(pair), real(pair), real(LANES), meta(pair), meta(pair), meta(LANES)],
        out_specs=[real(LANES), meta(LANES)],
        out_shape=[jax.ShapeDtypeStruct((b, seq, MLA_HEADS * MLA_V), BF16),
                   jax.ShapeDtypeStruct((b, N_META, MLA_HEADS * MLA_V), BF16)],
        scratch_shapes=[pltpu.VMEM((seq, pair), BF16), pltpu.VMEM((LANES, pair), BF16),
                        pltpu.VMEM((LANES, pair), BF16), pltpu.VMEM((2 * N_META, pair), BF16),
                        pltpu.VMEM((2, tq, pair), F32), pltpu.VMEM((2, tq, 1), F32)],
        compiler_params=pltpu.CompilerParams(dimension_semantics=("parallel", "parallel"),
                                             vmem_limit_bytes=VMEM_LIMIT_BYTES),
        name="mla_attn",
    )(q_r, k_r, v_r, q_m, k_m, v_m)


def _layer_norm(x, g, b):
    mu = jnp.mean(x, axis=-1, keepdims=True)
    xc = x - mu
    var = jnp.mean(xc * xc, axis=-1, keepdims=True)
    return xc * lax.rsqrt(var + LN_EPS) * g + b


def _post_kernel(a_ref, h_ref, wo_ref, g1_ref, b1_ref, w1_ref, w2_ref, g2_ref, b2_ref, o_ref, *, alpha, ff_chunk):
    mix = _dot(a_ref[...], wo_ref[...])
    h1 = _layer_norm(alpha * h_ref[...] + mix, g1_ref[...], b1_ref[...])
    h1b = h1.astype(BF16)
    acc = jnp.zeros(h1.shape, F32)
    for c in range(D_FF // ff_chunk):
        sl = slice(c * ff_chunk, (c + 1) * ff_chunk)
        u = jnp.maximum(_dot(h1b, w1_ref[:, sl]), 0.0)
        acc = acc + _dot((u * u).astype(BF16), w2_ref[sl, :])
    o_ref[...] = _layer_norm(alpha * h1 + acc, g2_ref[...], b2_ref[...])


def _post(a2d, h2d, w, alpha, tm):
    rows = a2d.shape[0]
    row_spec = pl.BlockSpec((tm, D_MODEL), lambda i: (i, 0))
    names = ('wo', 'g1', 'b1', 'w1', 'w2', 'g2', 'b2')
    return pl.pallas_call(
        functools.partial(_post_kernel, alpha=alpha, ff_chunk=1024),
        grid=(rows // tm,),
        in_specs=[row_spec, row_spec] + [_const_spec(w[n].shape) for n in names],
        out_specs=row_spec,
        out_shape=jax.ShapeDtypeStruct((rows, D_MODEL), F32),
        compiler_params=pltpu.CompilerParams(dimension_semantics=("parallel",),
                                             vmem_limit_bytes=VMEM_LIMIT_BYTES),
        name="post_mlp",
    )(a2d, h2d, *[w[n] for n in names])


def _rope_block(blk, c, s1, s2):
    return blk * c + pltpu.roll(blk, LANES - GQA_ROT // 2, 1) * s1 + pltpu.roll(blk, GQA_ROT // 2, 1) * s2


def _gqa_proj_kernel(*refs, with_q):
    if with_q:
        h_ref, kc_ref, ks1_ref, ks2_ref, wk_ref, wv_ref, qc_ref, qs1_ref, qs2_ref, wq_ref, k_ref, v_ref, q_ref = refs
    else:
        h_ref, kc_ref, ks1_ref, ks2_ref, wk_ref, wv_ref, k_ref, v_ref = refs
    hb = h_ref[...].astype(BF16)
    if with_q:
        q = _dot(hb, wq_ref[...])
        qc, qs1, qs2 = qc_ref[...], qs1_ref[...], qs2_ref[...]
        for j in range(q.shape[1] // LANES):
            sl = slice(j * LANES, (j + 1) * LANES)
            q_ref[:, sl] = _rope_block(q[:, sl], qc, qs1, qs2).astype(BF16)
    k = _dot(hb, wk_ref[...])
    v = _dot(hb, wv_ref[...])
    kc, ks1, ks2 = kc_ref[...], ks1_ref[...], ks2_ref[...]
    first = lax.broadcasted_iota(jnp.int32, (1, LANES), 1) < GQA_HEAD_DIM
    ones = jnp.ones((k.shape[0], LANES), BF16)
    for j in range(GQA_KV_HEADS // 2):
        sl = slice(j * LANES, (j + 1) * LANES)
        kb = _rope_block(k[:, sl], kc, ks1, ks2)
        vb = v[:, sl]
        ksw = pltpu.roll(kb, GQA_HEAD_DIM, 1)
        vsw = pltpu.roll(vb, GQA_HEAD_DIM, 1)
        for g, (klo, khi, vlo, vhi) in ((2 * j, (kb, ksw, vb, vsw)), (2 * j + 1, (ksw, kb, vsw, vb))):
            k_ref[:, (2 * g) * LANES:(2 * g + 1) * LANES] = jnp.where(first, klo, 0.0).astype(BF16)
            k_ref[:, (2 * g + 1) * LANES:(2 * g + 2) * LANES] = jnp.where(first, 0.0, khi).astype(BF16)
            v_ref[:, (4 * g) * LANES:(4 * g + 1) * LANES] = jnp.where(first, vlo, 0.0).astype(BF16)
            v_ref[:, (4 * g + 1) * LANES:(4 * g + 2) * LANES] = ones
            v_ref[:, (4 * g + 2) * LANES:(4 * g + 3) * LANES] = jnp.where(first, 0.0, vhi).astype(BF16)
            v_ref[:, (4 * g + 3) * LANES:(4 * g + 4) * LANES] = ones


def _gqa_proj(h2d, tabs, w, tm, with_q):
    rows = h2d.shape[0]
    nper = tabs['kc'].shape[0] // tm
    row_spec = lambda n: pl.BlockSpec((tm, n), lambda i: (i, 0))
    tab_spec = pl.BlockSpec((tm, LANES), lambda i: (i % nper, 0))
    nq = GQA_Q_HEADS * GQA_HEAD_DIM
    nk = GQA_KV_HEADS * 2 * LANES
    nv = GQA_KV_HEADS * 4 * LANES
    args = [h2d, tabs['kc'], tabs['ks1'], tabs['ks2'], w['wk'], w['wv']]
    in_specs = [row_spec(D_MODEL)] + [tab_spec] * 3 + [_const_spec(w['wk'].shape), _const_spec(w['wv'].shape)]
    out_specs = [row_spec(nk), row_spec(nv)]
    out_shape = [jax.ShapeDtypeStruct((rows, nk), BF16), jax.ShapeDtypeStruct((rows, nv), BF16)]
    if with_q:
        args += [tabs['qc'], tabs['qs1'], tabs['qs2'], w['wq']]
        in_specs += [tab_spec] * 3 + [_const_spec(w['wq'].shape)]
        out_specs.append(row_spec(nq))
        out_shape.append(jax.ShapeDtypeStruct((rows, nq), BF16))
    return pl.pallas_call(
        functools.partial(_gqa_proj_kernel, with_q=with_q),
        grid=(rows // tm,),
        in_specs=in_specs,
        out_specs=out_specs,
        out_shape=out_shape,
        compiler_params=pltpu.CompilerParams(dimension_semantics=("parallel",),
                                             vmem_limit_bytes=VMEM_LIMIT_BYTES),
        name="gqa_proj",
    )(*args)


def _gqa_attn_kernel(sink_ref, q_ref, kp_ref, kc_ref, kn_ref, vp_ref, vc_ref, vn_ref, km_ref, vm_ref, o_ref,
                     kw_ref, vw_ref, *, tq):
    i = pl.program_id(1)
    nsub = tq // BLOCK
    nblk = pl.num_programs(1) * nsub
    mrow = (nsub - 1) * BLOCK
    for src, dst in ((km_ref, kw_ref), (vm_ref, vw_ref)):
        dst[mrow:mrow + BLOCK, :] = jnp.zeros((BLOCK, dst.shape[1]), BF16)
        dst[mrow:mrow + N_META, :] = src[0]
    for prv, cur, nxt, dst in ((kp_ref, kc_ref, kn_ref, kw_ref), (vp_ref, vc_ref, vn_ref, vw_ref)):
        dst[0:BLOCK, :] = prv[0]
        dst[BLOCK:mrow, :] = cur[0, 0:mrow - BLOCK, :]
        dst[mrow + BLOCK:tq + 2 * BLOCK, :] = cur[0, mrow - BLOCK:tq, :]
        dst[tq + 2 * BLOCK:tq + 3 * BLOCK, :] = nxt[0]

    span = 4 * BLOCK
    rows2 = 2 * BLOCK
    qi = lax.broadcasted_iota(jnp.int32, (rows2, 1), 0) % BLOCK
    col = lax.broadcasted_iota(jnp.int32, (1, span), 1)
    col_blk = col // BLOCK
    col_in = col % BLOCK
    first_head = lax.broadcasted_iota(jnp.int32, (1, LANES), 1) < GQA_HEAD_DIM
    upper_rows = lax.broadcasted_iota(jnp.int32, (rows2, 1), 0) >= BLOCK
    always = 1 << 30

    def sub_block(s, carry):
        q0 = pl.multiple_of(s * BLOCK, BLOCK)
        blk = i * nsub + s
        mpos = nsub - 1 - s
        is_meta = col_blk == mpos
        rel = col_blk - jnp.where(col_blk > mpos, 1, 0)
        dead = ((rel == 0) & (blk == 0)) | ((rel == 2) & (blk == nblk - 1))
        centre = jnp.where(is_meta, 0, (rel - 1) * BLOCK + col_in)
        reach = jnp.where(is_meta, jnp.where(col_in < N_META, always, -1), jnp.where(dead, -1, WINDOW))
        mask = jnp.abs(centre - qi) <= reach
        for g in range(GQA_KV_HEADS):
            qa = q_ref[0, pl.ds(q0, BLOCK), (2 * g) * LANES:(2 * g + 1) * LANES]
            qb = q_ref[0, pl.ds(q0, BLOCK), (2 * g + 1) * LANES:(2 * g + 2) * LANES]
            lhs = jnp.concatenate([qa, qb], axis=0)
            outs, invs = [], []
            for par in range(2):
                kl = (2 * g + par) * LANES
                vl = (2 * g + par) * 2 * LANES
                sc = jnp.where(mask, _nt_dot(lhs, kw_ref[pl.ds(q0, span), kl:kl + LANES]), NEG_INF)
                sink = jnp.where(upper_rows, sink_ref[4 * g + 2 + par], sink_ref[4 * g + par])
                m = jnp.maximum(jnp.max(sc, axis=-1, keepdims=True), sink)
                p = jnp.exp2(sc - m).astype(BF16)
                o = _dot(p, vw_ref[pl.ds(q0, span), vl:vl + 2 * LANES])
                outs.append(o[:, :LANES])
                invs.append(1.0 / (o[:, LANES:] + jnp.exp2(sink - m)))
            res = ((outs[0] + outs[1]) * jnp.where(first_head, invs[0], invs[1])).astype(BF16)
            o_ref[0, pl.ds(q0, BLOCK), (2 * g) * LANES:(2 * g + 1) * LANES] = res[:BLOCK]
            o_ref[0, pl.ds(q0, BLOCK), (2 * g + 1) * LANES:(2 * g + 2) * LANES] = res[BLOCK:]
        return carry

    lax.fori_loop(0, nsub, sub_block, 0)


def _gqa_attn(sink2, q, k, v, k_m, v_m):
    b, seq, nq = q.shape
    nk, nv = k.shape[2], v.shape[2]
    tq = 4 * BLOCK
    assert seq % tq == 0
    nsub = tq // BLOCK
    nblk = seq // BLOCK
    tile = lambda n: pl.BlockSpec((1, tq, n), lambda bi, i: (bi, i, 0))
    prev = lambda n: pl.BlockSpec((1, BLOCK, n), lambda bi, i: (bi, jnp.maximum(i * nsub - 1, 0), 0))
    nxt = lambda n: pl.BlockSpec((1, BLOCK, n), lambda bi, i: (bi, jnp.minimum((i + 1) * nsub, nblk - 1), 0))
    meta = lambda n: pl.BlockSpec((1, N_META, n), lambda bi, i: (bi, 0, 0))
    return pl.pallas_call(
        functools.partial(_gqa_attn_kernel, tq=tq),
        grid=(b, seq // tq),
        in_specs=[pl.BlockSpec(memory_space=pltpu.SMEM), tile(nq), prev(nk), tile(nk), nxt(nk),
                  prev(nv), tile(nv), nxt(nv), meta(nk), meta(nv)],
        out_specs=tile(nq),
        out_shape=jax.ShapeDtypeStruct((b, seq, nq), BF16),
        scratch_shapes=[pltpu.VMEM((tq + 3 * BLOCK, nk), BF16), pltpu.VMEM((tq + 3 * BLOCK, nv), BF16)],
        compiler_params=pltpu.CompilerParams(dimension_semantics=("parallel", "parallel"),
                                             vmem_limit_bytes=VMEM_LIMIT_BYTES),
        name="gqa_attn",
    )(sink2, q, k, k, k, v, v, v, k_m, v_m)


def _swap_halves(w):
    half = w.shape[-1] // 2
    return jnp.concatenate([-w[..., half:], w[..., :half]], axis=-1)


def _prep_mla(w_in, g_q, w_uq, g_kv, w_ukv, w_o):
    d = w_in.shape[0]
    z64 = jnp.zeros((d, MLA_NOPE), F32)
    wkr = w_in[:, MLA_Q_RANK + MLA_KV_RANK:]
    wkr_sw = _swap_halves(wkr)
    w_in_x = jnp.concatenate([w_in[:, :MLA_Q_RANK + MLA_KV_RANK], z64, wkr, wkr, z64, wkr_sw, wkr_sw], axis=1)
    wq = w_uq.reshape(MLA_Q_RANK, MLA_HEADS, MLA_NOPE + MLA_ROPE)
    wq = jnp.concatenate([wq, _swap_halves(wq[..., MLA_NOPE:])], axis=-1).reshape(MLA_Q_RANK, MLA_HEADS * LANES)
    wkv = w_ukv.reshape(MLA_KV_RANK, MLA_HEADS, MLA_NOPE + MLA_V)
    wk = jnp.concatenate([wkv[..., :MLA_NOPE], jnp.zeros_like(wkv[..., :MLA_NOPE])], axis=-1)
    wk = wk.reshape(MLA_KV_RANK, MLA_HEADS * LANES)
    wv = wkv[..., MLA_NOPE:].reshape(MLA_KV_RANK, MLA_HEADS * MLA_V)
    return dict(w_in=w_in_x.astype(BF16), g_q=g_q.reshape(1, -1), g_kv=g_kv.reshape(1, -1),
                wq=wq.astype(BF16), wk=wk.astype(BF16), wv=wv.astype(BF16)), w_o.astype(BF16)


def _prep_gqa(w_qkv, w_o):
    qd = GQA_Q_HEADS * GQA_HEAD_DIM
    kd = GQA_KV_HEADS * GQA_HEAD_DIM
    return dict(wq=w_qkv[:, :qd].astype(BF16), wk=w_qkv[:, qd:qd + kd].astype(BF16),
                wv=w_qkv[:, qd + kd:].astype(BF16)), w_o.astype(BF16)


def _rope_angles(length, dim):
    pos = jnp.arange(length, dtype=F32)
    inv = ROPE_THETA ** (-jnp.arange(0, dim, 2, dtype=F32) / dim)
    ang = pos[:, None] * inv[None, :]
    return jnp.cos(ang), jnp.sin(ang)


def _mla_tables(length):
    cos, sin = _rope_angles(length, MLA_ROPE)
    c = jnp.concatenate([cos, cos], axis=1)
    s = jnp.concatenate([sin, sin], axis=1)
    one = jnp.ones((length, MLA_NOPE), F32)
    zero = jnp.zeros((length, MLA_NOPE), F32)
    scale = (MLA_NOPE + MLA_ROPE) ** -0.5 * LOG2E
    return dict(tq=jnp.concatenate([one, c, s], axis=1) * scale,
                ta=jnp.concatenate([zero, c, c], axis=1), tb=jnp.concatenate([zero, s, s], axis=1))


def _gqa_tables(length):
    cos, sin = _rope_angles(length, GQA_ROT)
    rest = GQA_HEAD_DIM - GQA_ROT
    zr = jnp.zeros((length, rest), F32)
    z8 = jnp.zeros_like(sin)
    c = jnp.concatenate([cos, cos, jnp.ones((length, rest), F32)], axis=1)
    s1 = jnp.concatenate([-sin, z8, zr], axis=1)
    s2 = jnp.concatenate([z8, sin, zr], axis=1)
    two = lambda t: jnp.concatenate([t, t], axis=1)
    scale = GQA_HEAD_DIM ** -0.5 * LOG2E
    return dict(kc=two(c), ks1=two(s1), ks2=two(s2), qc=two(c) * scale, qs1=two(s1) * scale, qs2=two(s2) * scale)


def _split_tables(tabs, batch):
    real = {n: t[N_META:] for n, t in tabs.items()}
    meta = {n: jnp.tile(t[:N_META], (batch, 1)) for n, t in tabs.items()}
    return real, meta


def _trunk(x, meta_tokens, layers, tile_rows=512):
    b, seq, d = x.shape
    depth = len(layers)
    alpha = (2.0 * depth) ** 0.25
    length = seq + N_META
    h_r = x.reshape(b * seq, d)
    h_m = jnp.broadcast_to(meta_tokens[None].astype(x.dtype), (b, N_META, d)).reshape(b * N_META, d)
    mla_r, mla_m = _split_tables(_mla_tables(length), b)
    gqa_r, gqa_m = _split_tables(_gqa_tables(length), b)
    tm_r = _row_tile(seq, tile_rows)
    tm_m = _row_tile(b * N_META, tile_rows)
    shp = lambda t, n: t.reshape(b, n, t.shape[-1])
    for i, layer in enumerate(layers):
        last = i == depth - 1
        if layer['kind'] == 'mla':
            q_r, k_r, v_r = _mla_proj(h_r, mla_r, layer['proj'], tm_r)
            q_m, k_m, v_m = _mla_proj(h_m, mla_m, layer['proj'], tm_m)
            a_r, a_m = _mla_attn(shp(q_r, seq), shp(k_r, seq), shp(v_r, seq),
                                 shp(q_m, N_META), shp(k_m, N_META), shp(v_m, N_META))
        else:
            assert last, "windowed layer computes real-token outputs only"
            k_r, v_r, q_r = _gqa_proj(h_r, gqa_r, layer['proj'], tm_r, True)
            k_m, v_m = _gqa_proj(h_m, gqa_m, layer['proj'], tm_m, False)
            a_r = _gqa_attn(layer['sink2'], shp(q_r, seq), shp(k_r, seq), shp(v_r, seq),
                            shp(k_m, N_META), shp(v_m, N_META))
            a_m = None
        h_r = _post(a_r.reshape(b * seq, d), h_r, layer['post'], alpha, tm_r)
        if not last:
            h_m = _post(a_m.reshape(b * N_META, d), h_m, layer['post'], alpha, tm_m)
    return h_r.reshape(b, seq, d)


def kernel(x_prompt, x_sample, meta_tokens, mla_w_in, mla_g_q, mla_w_uq, mla_g_kv, mla_w_ukv, mla_w_o,
           gqa_w_qkv, gqa_sink, gqa_w_o, mlp_w1, mlp_w2, ln1_g, ln1_b, ln2_g, ln2_b):
    depth = mlp_w1.shape[0]
    layers = []
    for i in range(depth):
        j = i // 2
        if i % 2 == 0:
            proj, wo = _prep_mla(mla_w_in[j], mla_g_q[j], mla_w_uq[j], mla_g_kv[j], mla_w_ukv[j], mla_w_o[j])
            layer = dict(kind='mla', proj=proj)
        else:
            proj, wo = _prep_gqa(gqa_w_qkv[j], gqa_w_o[j])
            layer = dict(kind='gqa', proj=proj, sink2=gqa_sink[j].astype(F32) * LOG2E)
        row = lambda t: t[i].reshape(1, -1).astype(F32)
        layer['post'] = dict(wo=wo, g1=row(ln1_g), b1=row(ln1_b), w1=mlp_w1[i].astype(BF16),
                             w2=mlp_w2[i].astype(BF16), g2=row(ln2_g), b2=row(ln2_b))
        layers.append(layer)
    return (_trunk(x_prompt, meta_tokens, layers), _trunk(x_sample, meta_tokens, layers))
```

```python
import functools
import math

import jax
import jax.numpy as jnp
from jax import lax
from jax.experimental import pallas as pl
from jax.experimental.pallas import tpu as pltpu

D_MODEL = 1024
N_META = 16
BLOCK = 128
WINDOW = 128
ROPE_THETA = 500000.0
MLA_HEADS = 16
MLA_NOPE = 64
MLA_ROPE = 32
MLA_V = 64
MLA_KV_RANK = 256
MLA_Q_RANK = 768
GQA_Q_HEADS = 16
GQA_KV_HEADS = 4
GQA_HEAD_DIM = 64
GQA_ROT = 16
D_FF = 4096
LN_EPS = 1e-5
RMS_EPS = 1e-6
NEG_INF = -1e30
LOG2E = math.log2(math.e)

LANES = 128
VMEM_LIMIT_BYTES = 56 * 1024 * 1024

BF16 = jnp.bfloat16
F32 = jnp.float32


def _row_tile(rows, target):
    if rows <= target:
        return rows
    t = target
    while rows % t:
        t //= 2
    return t


def _const_spec(shape):
    nd = len(shape)
    return pl.BlockSpec(shape, lambda *_: (0,) * nd, pipeline_mode=pl.Buffered(1))


def _nt_dot(a, b):
    return lax.dot_general(a, b, (((1,), (1,)), ((), ())), preferred_element_type=F32)


def _dot(a, b):
    return jnp.dot(a, b, preferred_element_type=F32)


def _mla_proj_kernel(x_ref, tq_ref, ta_ref, tb_ref, w_in_ref, gq_ref, gkv_ref, wq_ref, wk_ref, wv_ref,
                     q_ref, k_ref, v_ref):
    xb = x_ref[...].astype(BF16)
    c = _dot(xb, w_in_ref[...])
    cq = c[:, :MLA_Q_RANK]
    cq = cq * lax.rsqrt(jnp.mean(cq * cq, axis=-1, keepdims=True) + RMS_EPS) * gq_ref[...]
    kv0 = MLA_Q_RANK
    ckv = c[:, kv0:kv0 + MLA_KV_RANK]
    ckv = ckv * lax.rsqrt(jnp.mean(ckv * ckv, axis=-1, keepdims=True) + RMS_EPS) * gkv_ref[...]
    r0 = kv0 + MLA_KV_RANK
    kr = c[:, r0:r0 + LANES] * ta_ref[...] + c[:, r0 + LANES:r0 + 2 * LANES] * tb_ref[...]
    cqb = cq.astype(BF16)
    ckvb = ckv.astype(BF16)
    q = _dot(cqb, wq_ref[...])
    k = _dot(ckvb, wk_ref[...])
    tq = tq_ref[...]
    for h in range(MLA_HEADS):
        sl = slice(h * LANES, (h + 1) * LANES)
        q_ref[:, sl] = (q[:, sl] * tq).astype(BF16)
        k_ref[:, sl] = (k[:, sl] + kr).astype(BF16)
    v_ref[...] = _dot(ckvb, wv_ref[...]).astype(BF16)


def _mla_proj(x2d, tabs, w, tm):
    rows = x2d.shape[0]
    nper = tabs['tq'].shape[0] // tm
    row_spec = lambda n: pl.BlockSpec((tm, n), lambda i: (i, 0))
    tab_spec = pl.BlockSpec((tm, LANES), lambda i: (i % nper, 0))
    hq = MLA_HEADS * LANES
    return pl.pallas_call(
        _mla_proj_kernel,
        grid=(rows // tm,),
        in_specs=[row_spec(D_MODEL), tab_spec, tab_spec, tab_spec,
                  _const_spec(w['w_in'].shape), _const_spec(w['g_q'].shape), _const_spec(w['g_kv'].shape),
                  _const_spec(w['wq'].shape), _const_spec(w['wk'].shape), _const_spec(w['wv'].shape)],
        out_specs=[row_spec(hq), row_spec(hq), row_spec(MLA_HEADS * MLA_V)],
        out_shape=[jax.ShapeDtypeStruct((rows, hq), BF16), jax.ShapeDtypeStruct((rows, hq), BF16),
                   jax.ShapeDtypeStruct((rows, MLA_HEADS * MLA_V), BF16)],
        compiler_params=pltpu.CompilerParams(dimension_semantics=("parallel",),
                                             vmem_limit_bytes=VMEM_LIMIT_BYTES),
        name="mla_proj",
    )(x2d, tabs['tq'], tabs['ta'], tabs['tb'], w['w_in'], w['g_q'], w['g_kv'], w['wq'], w['wk'], w['wv'])


MLA_Q_ROWS = 1024
MLA_KEY_BLOCK = 2048


def _mla_attn_kernel(qr_ref, kr_ref, vr_ref, qm_ref, km_ref, vm_ref, or_ref, om_ref,
                     vxr_ref, vxm_ref, kms_ref, qmx_ref, acc_ref, m_ref, *, tq, kb):
    seq = qr_ref.shape[1]
    pair = 2 * LANES
    nkb = seq // kb
    vxr_ref[:, :LANES] = vr_ref[0]
    vxr_ref[:, LANES:] = jnp.ones((seq, LANES), BF16)
    vxm_ref[...] = jnp.zeros(vxm_ref.shape, BF16)
    vxm_ref[:N_META, :LANES] = vm_ref[0]
    vxm_ref[:N_META, LANES:] = jnp.ones((N_META, LANES), BF16)
    kms_ref[...] = jnp.zeros(kms_ref.shape, BF16)
    kms_ref[:N_META, :] = km_ref[0]
    qmx_ref[...] = jnp.zeros(qmx_ref.shape, BF16)
    for t in range(2):
        qmx_ref[t * N_META:(t + 1) * N_META, t * LANES:(t + 1) * LANES] = qm_ref[0, :, t * LANES:(t + 1) * LANES]
    lane = lax.broadcasted_iota(jnp.int32, (1, LANES), 1)
    meta_valid = lane < N_META
    first_head = lane < MLA_V

    def attend(q, feat):
        parts = []
        for j in range(nkb):
            rows = slice(j * kb, (j + 1) * kb)
            s = _nt_dot(q, kr_ref[0, rows, feat])
            m = jnp.max(s, axis=-1, keepdims=True)
            if j == nkb - 1:
                s_m = jnp.where(meta_valid, _nt_dot(q, kms_ref[:, feat]), NEG_INF)
                m = jnp.maximum(m, jnp.max(s_m, axis=-1, keepdims=True))
            o = _dot(jnp.exp2(s - m).astype(BF16), vxr_ref[rows, :])
            if j == nkb - 1:
                o = o + _dot(jnp.exp2(s_m - m).astype(BF16), vxm_ref[...])
            parts.append((m, o))
        m_all, o = parts[0]
        if nkb > 1:
            for m_j, _ in parts[1:]:
                m_all = jnp.maximum(m_all, m_j)
            o = sum(o_j * jnp.exp2(m_j - m_all) for m_j, o_j in parts)
        return o[:, :LANES] / o[:, LANES:]

    def chunk(ci, carry):
        r0 = pl.multiple_of(ci * tq, tq)
        res = [attend(qr_ref[0, pl.ds(r0, tq), t * LANES:(t + 1) * LANES], slice(t * LANES, (t + 1) * LANES))
               for t in range(2)]
        or_ref[0, pl.ds(r0, tq), :] = jnp.where(first_head, res[0], res[1]).astype(BF16)
        return carry

    def chunk_block(it, carry):
        ci, j = it // nkb, it % nkb
        r0 = pl.multiple_of(ci * tq, tq)
        k0 = pl.multiple_of(j * kb, kb)
        first_block = j == 0
        with_meta = jnp.where(j == nkb - 1, 1.0, 0.0)
        res = []
        for t in range(2):
            feat = slice(t * LANES, (t + 1) * LANES)
            q = qr_ref[0, pl.ds(r0, tq), feat]
            s = _nt_dot(q, kr_ref[0, pl.ds(k0, kb), feat])
            s_m = jnp.where(meta_valid, _nt_dot(q, kms_ref[:, feat]), NEG_INF)
            m_prev = jnp.where(first_block, NEG_INF, m_ref[t])
            m = jnp.maximum(jnp.maximum(m_prev, jnp.max(s, axis=-1, keepdims=True)),
                            jnp.max(s_m, axis=-1, keepdims=True))
            o = (acc_ref[t] * jnp.exp2(m_prev - m) + _dot(jnp.exp2(s - m).astype(BF16), vxr_ref[pl.ds(k0, kb), :]) +
                 _dot((jnp.exp2(s_m - m) * with_meta).astype(BF16), vxm_ref[...]))
            acc_ref[t] = o
            m_ref[t] = m
            res.append(o[:, :LANES] / o[:, LANES:])
        or_ref[0, pl.ds(r0, tq), :] = jnp.where(first_head, res[0], res[1]).astype(BF16)
        return carry

    if nkb == 1:
        lax.fori_loop(0, seq // tq, chunk, 0)
    else:
        acc_ref[...] = jnp.zeros(acc_ref.shape, F32)
        lax.fori_loop(0, (seq // tq) * nkb, chunk_block, 0)
    res = attend(qmx_ref[...], slice(0, pair))
    om_ref[0] = jnp.where(first_head, res[:N_META], res[N_META:]).astype(BF16)


def _mla_attn(q_r, k_r, v_r, q_m, k_m, v_m):
    b, seq, _ = q_r.shape
    kb = _row_tile(seq, MLA_KEY_BLOCK)
    tq = _row_tile(seq, MLA_Q_ROWS if kb == seq else MLA_Q_ROWS // 2)
    pair = 2 * LANES
    real = lambda n: pl.BlockSpec((1, seq, n), lambda i, j: (i, 0, j))
    meta = lambda n: pl.BlockSpec((1, N_META, n), lambda i, j: (i, 0, j))
    return pl.pallas_call(
        functools.partial(_mla_attn_kernel, tq=tq, kb=kb),
        grid=(b, MLA_HEADS // 2),
        in_specs=[real(pair), real(pair), real(LANES), meta(pair), meta(pair), meta(LANES)],
        out_specs=[real(LANES), meta(LANES)],
        out_shape=[jax.ShapeDtypeStruct((b, seq, MLA_HEADS * MLA_V), BF16),
                   jax.ShapeDtypeStruct((b, N_META, MLA_HEADS * MLA_V), BF16)],
        scratch_shapes=[pltpu.VMEM((seq, pair), BF16), pltpu.VMEM((LANES, pair), BF16),
                        pltpu.VMEM((LANES, pair), BF16), pltpu.VMEM((2 * N_META, pair), BF16),
                        pltpu.VMEM((2, tq, pair), F32), pltpu.VMEM((2, tq, 1), F32)],
        compiler_params=pltpu.CompilerParams(dimension_semantics=("parallel", "parallel"),
                                             vmem_limit_bytes=VMEM_LIMIT_BYTES),
        name="mla_attn",
    )(q_r, k_r, v_r, q_m, k_m, v_m)


POST_TILE_ROWS = 512
POST_SUB_ROWS = 256


def _layer_norm(x, g, b):
    mu = jnp.mean(x, axis=-1, keepdims=True)
    xc = x - mu
    var = jnp.mean(xc * xc, axis=-1, keepdims=True)
    return xc * lax.rsqrt(var + LN_EPS) * g + b


def _post_kernel(a_ref, h_ref, wo_ref, g1_ref, b1_ref, w1_ref, w2_ref, g2_ref, b2_ref, o_ref, *,
                 alpha, ff_chunk, sub_rows):
    subs = [slice(r * sub_rows, (r + 1) * sub_rows) for r in range(a_ref.shape[0] // sub_rows)]
    mixes = [_dot(a_ref[rows, :], wo_ref[...]) for rows in subs]
    pending = None
    for rows, mix in zip(subs, mixes):
        h1 = _layer_norm(alpha * h_ref[rows, :] + mix, g1_ref[...], b1_ref[...])
        h1b = h1.astype(BF16)
        acc = jnp.zeros(h1.shape, F32)
        for c in range(D_FF // ff_chunk):
            sl = slice(c * ff_chunk, (c + 1) * ff_chunk)
            u = jnp.maximum(_dot(h1b, w1_ref[:, sl]), 0.0)
            acc = acc + _dot((u * u).astype(BF16), w2_ref[sl, :])
        if pending is not None:
            p_rows, p_h1, p_acc = pending
            o_ref[p_rows, :] = _layer_norm(alpha * p_h1 + p_acc, g2_ref[...], b2_ref[...])
        pending = (rows, h1, acc)
    p_rows, p_h1, p_acc = pending
    o_ref[p_rows, :] = _layer_norm(alpha * p_h1 + p_acc, g2_ref[...], b2_ref[...])


def _post(a2d, h2d, w, alpha, tm):
    rows = a2d.shape[0]
    row_spec = pl.BlockSpec((tm, D_MODEL), lambda i: (i, 0))
    names = ('wo', 'g1', 'b1', 'w1', 'w2', 'g2', 'b2')
    return pl.pallas_call(
        functools.partial(_post_kernel, alpha=alpha, ff_chunk=1024, sub_rows=min(tm, POST_SUB_ROWS)),
        grid=(rows // tm,),
        in_specs=[row_spec, row_spec] + [_const_spec(w[n].shape) for n in names],
        out_specs=row_spec,
        out_shape=jax.ShapeDtypeStruct((rows, D_MODEL), F32),
        compiler_params=pltpu.CompilerParams(dimension_semantics=("parallel",),
                                             vmem_limit_bytes=VMEM_LIMIT_BYTES),
        name="post_mlp",
    )(a2d, h2d, *[w[n] for n in names])


def _rope_block(blk, c, s1, s2):
    return blk * c + pltpu.roll(blk, LANES - GQA_ROT // 2, 1) * s1 + pltpu.roll(blk, GQA_ROT // 2, 1) * s2


def _gqa_proj_kernel(*refs, with_q):
    if with_q:
        h_ref, kc_ref, ks1_ref, ks2_ref, wk_ref, wv_ref, qc_ref, qs1_ref, qs2_ref, wq_ref, k_ref, v_ref, q_ref = refs
    else:
        h_ref, kc_ref, ks1_ref, ks2_ref, wk_ref, wv_ref, k_ref, v_ref = refs
    hb = h_ref[...].astype(BF16)
    if with_q:
        q = _dot(hb, wq_ref[...])
        qc, qs1, qs2 = qc_ref[...], qs1_ref[...], qs2_ref[...]
        for j in range(q.shape[1] // LANES):
            sl = slice(j * LANES, (j + 1) * LANES)
            q_ref[:, sl] = _rope_block(q[:, sl], qc, qs1, qs2).astype(BF16)
    k = _dot(hb, wk_ref[...])
    v = _dot(hb, wv_ref[...])
    kc, ks1, ks2 = kc_ref[...], ks1_ref[...], ks2_ref[...]
    first = lax.broadcasted_iota(jnp.int32, (1, LANES), 1) < GQA_HEAD_DIM
    ones = jnp.ones((k.shape[0], LANES), BF16)
    for j in range(GQA_KV_HEADS // 2):
        sl = slice(j * LANES, (j + 1) * LANES)
        kb = _rope_block(k[:, sl], kc, ks1, ks2)
        vb = v[:, sl]
        ksw = pltpu.roll(kb, GQA_HEAD_DIM, 1)
        vsw = pltpu.roll(vb, GQA_HEAD_DIM, 1)
        for g, (kx, ky, vx, vy) in ((2 * j, (kb, ksw, vb, vsw)), (2 * j + 1, (ksw, kb, vsw, vb))):
            k_ref[:, g * LANES:(g + 1) * LANES] = jnp.where(first, kx, ky).astype(BF16)
            v_ref[:, (2 * g) * LANES:(2 * g + 1) * LANES] = jnp.where(first, vx, vy).astype(BF16)
            v_ref[:, (2 * g + 1) * LANES:(2 * g + 2) * LANES] = ones


def _gqa_proj(h2d, tabs, w, tm, with_q):
    rows = h2d.shape[0]
    nper = tabs['kc'].shape[0] // tm
    row_spec = lambda n: pl.BlockSpec((tm, n), lambda i: (i, 0))
    tab_spec = pl.BlockSpec((tm, LANES), lambda i: (i % nper, 0))
    nq = GQA_Q_HEADS * GQA_HEAD_DIM
    nk = GQA_KV_HEADS * LANES
    nv = GQA_KV_HEADS * 2 * LANES
    args = [h2d, tabs['kc'], tabs['ks1'], tabs['ks2'], w['wk'], w['wv']]
    in_specs = [row_spec(D_MODEL)] + [tab_spec] * 3 + [_const_spec(w['wk'].shape), _const_spec(w['wv'].shape)]
    out_specs = [row_spec(nk), row_spec(nv)]
    out_shape = [jax.ShapeDtypeStruct((rows, nk), BF16), jax.ShapeDtypeStruct((rows, nv), BF16)]
    if with_q:
        args += [tabs['qc'], tabs['qs1'], tabs['qs2'], w['wq']]
        in_specs += [tab_spec] * 3 + [_const_spec(w['wq'].shape)]
        out_specs.append(row_spec(nq))
        out_shape.append(jax.ShapeDtypeStruct((rows, nq), BF16))
    return pl.pallas_call(
        functools.partial(_gqa_proj_kernel, with_q=with_q),
        grid=(rows // tm,),
        in_specs=in_specs,
        out_specs=out_specs,
        out_shape=out_shape,
        compiler_params=pltpu.CompilerParams(dimension_semantics=("parallel",),
                                             vmem_limit_bytes=VMEM_LIMIT_BYTES),
        name="gqa_proj",
    )(*args)


def _gqa_attn_kernel(sink_ref, q_ref, kp_ref, kc_ref, kn_ref, vp_ref, vc_ref, vn_ref, km_ref, vm_ref, o_ref,
                     kw_ref, vw_ref, *, tq):
    i = pl.program_id(1)
    nsub = tq // BLOCK
    nblk = pl.num_programs(1) * nsub
    mrow = (nsub - 1) * BLOCK
    for src, dst in ((km_ref, kw_ref), (vm_ref, vw_ref)):
        dst[mrow:mrow + BLOCK, :] = jnp.zeros((BLOCK, dst.shape[1]), BF16)
        dst[mrow:mrow + N_META, :] = src[0]
    for prv, cur, nxt, dst in ((kp_ref, kc_ref, kn_ref, kw_ref), (vp_ref, vc_ref, vn_ref, vw_ref)):
        dst[0:BLOCK, :] = prv[0]
        dst[BLOCK:mrow, :] = cur[0, 0:mrow - BLOCK, :]
        dst[mrow + BLOCK:tq + 2 * BLOCK, :] = cur[0, mrow - BLOCK:tq, :]
        dst[tq + 2 * BLOCK:tq + 3 * BLOCK, :] = nxt[0]

    span = 4 * BLOCK
    group = GQA_Q_HEADS // GQA_KV_HEADS
    qi = lax.broadcasted_iota(jnp.int32, (BLOCK, 1), 0)
    col = lax.broadcasted_iota(jnp.int32, (1, span), 1)
    col_blk = col // BLOCK
    col_in = col % BLOCK
    first_head = lax.broadcasted_iota(jnp.int32, (1, LANES), 1) < GQA_HEAD_DIM
    head_idx = lax.broadcasted_iota(jnp.int32, (group, 1, 1), 0)
    always = 1 << 30

    def sub_block(s, carry):
        q0 = pl.multiple_of(s * BLOCK, BLOCK)
        blk = i * nsub + s
        mpos = nsub - 1 - s
        is_meta = col_blk == mpos
        rel = col_blk - jnp.where(col_blk > mpos, 1, 0)
        dead = ((rel == 0) & (blk == 0)) | ((rel == 2) & (blk == nblk - 1))
        centre = jnp.where(is_meta, 0, (rel - 1) * BLOCK + col_in)
        reach = jnp.where(is_meta, jnp.where(col_in < N_META, always, -1), jnp.where(dead, -1, WINDOW))
        ceiling = jnp.where(jnp.abs(centre - qi) <= reach, jnp.inf, NEG_INF)[None]
        for g in range(GQA_KV_HEADS):
            lhs = []
            for pr in range(group // 2):
                qp = q_ref[0, pl.ds(q0, BLOCK), (2 * g + pr) * LANES:(2 * g + pr + 1) * LANES]
                lhs += [jnp.where(first_head, qp, jnp.zeros_like(qp)), jnp.where(first_head, jnp.zeros_like(qp), qp)]
            lhs = jnp.concatenate(lhs, axis=0)
            sc = _nt_dot(lhs, kw_ref[pl.ds(q0, span), g * LANES:(g + 1) * LANES])
            sc = jnp.minimum(sc.reshape(group, BLOCK, span), ceiling)
            sink = sink_ref[group * g + group - 1]
            for hh in range(group - 2, -1, -1):
                sink = jnp.where(head_idx == hh, sink_ref[group * g + hh], sink)
            m = jnp.maximum(jnp.max(sc, axis=-1, keepdims=True), sink)
            p = jnp.exp2(sc - m).astype(BF16).reshape(group * BLOCK, span)
            o = _dot(p, vw_ref[pl.ds(q0, span), 2 * g * LANES:2 * (g + 1) * LANES])
            o = o.reshape(group, BLOCK, 2 * LANES)
            res = o[:, :, :LANES] / (o[:, :, LANES:] + jnp.exp2(sink - m))
            for pr in range(group // 2):
                o_ref[0, pl.ds(q0, BLOCK), (2 * g + pr) * LANES:(2 * g + pr + 1) * LANES] = jnp.where(
                    first_head, res[2 * pr], res[2 * pr + 1]).astype(BF16)
        return carry

    lax.fori_loop(0, nsub, sub_block, 0)


def _gqa_attn(sink2, q, k, v, k_m, v_m):
    b, seq, nq = q.shape
    nk, nv = k.shape[2], v.shape[2]
    tq = 4 * BLOCK
    assert seq % tq == 0
    nsub = tq // BLOCK
    nblk = seq // BLOCK
    tile = lambda n: pl.BlockSpec((1, tq, n), lambda bi, i: (bi, i, 0))
    prev = lambda n: pl.BlockSpec((1, BLOCK, n), lambda bi, i: (bi, jnp.maximum(i * nsub - 1, 0), 0))
    nxt = lambda n: pl.BlockSpec((1, BLOCK, n), lambda bi, i: (bi, jnp.minimum((i + 1) * nsub, nblk - 1), 0))
    meta = lambda n: pl.BlockSpec((1, N_META, n), lambda bi, i: (bi, 0, 0))
    return pl.pallas_call(
        functools.partial(_gqa_attn_kernel, tq=tq),
        grid=(b, seq // tq),
        in_specs=[pl.BlockSpec(memory_space=pltpu.SMEM), tile(nq), prev(nk), tile(nk), nxt(nk),
                  prev(nv), tile(nv), nxt(nv), meta(nk), meta(nv)],
        out_specs=tile(nq),
        out_shape=jax.ShapeDtypeStruct((b, seq, nq), BF16),
        scratch_shapes=[pltpu.VMEM((tq + 3 * BLOCK, nk), BF16), pltpu.VMEM((tq + 3 * BLOCK, nv), BF16)],
        compiler_params=pltpu.CompilerParams(dimension_semantics=("parallel", "parallel"),
                                             vmem_limit_bytes=VMEM_LIMIT_BYTES),
        name="gqa_attn",
    )(sink2, q, k, k, k, v, v, v, k_m, v_m)


def _swap_halves(w):
    half = w.shape[-1] // 2
    return jnp.concatenate([-w[..., half:], w[..., :half]], axis=-1)


def _prep_mla(w_in, g_q, w_uq, g_kv, w_ukv, w_o):
    d = w_in.shape[0]
    z64 = jnp.zeros((d, MLA_NOPE), F32)
    wkr = w_in[:, MLA_Q_RANK + MLA_KV_RANK:]
    wkr_sw = _swap_halves(wkr)
    w_in_x = jnp.concatenate([w_in[:, :MLA_Q_RANK + MLA_KV_RANK], z64, wkr, wkr, z64, wkr_sw, wkr_sw], axis=1)
    wq = w_uq.reshape(MLA_Q_RANK, MLA_HEADS, MLA_NOPE + MLA_ROPE)
    wq = jnp.concatenate([wq, _swap_halves(wq[..., MLA_NOPE:])], axis=-1).reshape(MLA_Q_RANK, MLA_HEADS * LANES)
    wkv = w_ukv.reshape(MLA_KV_RANK, MLA_HEADS, MLA_NOPE + MLA_V)
    wk = jnp.concatenate([wkv[..., :MLA_NOPE], jnp.zeros_like(wkv[..., :MLA_NOPE])], axis=-1)
    wk = wk.reshape(MLA_KV_RANK, MLA_HEADS * LANES)
    wv = wkv[..., MLA_NOPE:].reshape(MLA_KV_RANK, MLA_HEADS * MLA_V)
    return dict(w_in=w_in_x.astype(BF16), g_q=g_q.reshape(1, -1), g_kv=g_kv.reshape(1, -1),
                wq=wq.astype(BF16), wk=wk.astype(BF16), wv=wv.astype(BF16)), w_o.astype(BF16)


def _prep_gqa(w_qkv, w_o):
    qd = GQA_Q_HEADS * GQA_HEAD_DIM
    kd = GQA_KV_HEADS * GQA_HEAD_DIM
    return dict(wq=w_qkv[:, :qd].astype(BF16), wk=w_qkv[:, qd:qd + kd].astype(BF16),
                wv=w_qkv[:, qd + kd:].astype(BF16)), w_o.astype(BF16)


def _rope_angles(length, dim):
    pos = jnp.arange(length, dtype=F32)
    inv = ROPE_THETA ** (-jnp.arange(0, dim, 2, dtype=F32) / dim)
    ang = pos[:, None] * inv[None, :]
    return jnp.cos(ang), jnp.sin(ang)


def _mla_tables(length):
    cos, sin = _rope_angles(length, MLA_ROPE)
    c = jnp.concatenate([cos, cos], axis=1)
    s = jnp.concatenate([sin, sin], axis=1)
    one = jnp.ones((length, MLA_NOPE), F32)
    zero = jnp.zeros((length, MLA_NOPE), F32)
    scale = (MLA_NOPE + MLA_ROPE) ** -0.5 * LOG2E
    return dict(tq=jnp.concatenate([one, c, s], axis=1) * scale,
                ta=jnp.concatenate([zero, c, c], axis=1), tb=jnp.concatenate([zero, s, s], axis=1))


def _gqa_tables(length):
    cos, sin = _rope_angles(length, GQA_ROT)
    rest = GQA_HEAD_DIM - GQA_ROT
    zr = jnp.zeros((length, rest), F32)
    z8 = jnp.zeros_like(sin)
    c = jnp.concatenate([cos, cos, jnp.ones((length, rest), F32)], axis=1)
    s1 = jnp.concatenate([-sin, z8, zr], axis=1)
    s2 = jnp.concatenate([z8, sin, zr], axis=1)
    two = lambda t: jnp.concatenate([t, t], axis=1)
    scale = GQA_HEAD_DIM ** -0.5 * LOG2E
    return dict(kc=two(c), ks1=two(s1), ks2=two(s2), qc=two(c) * scale, qs1=two(s1) * scale, qs2=two(s2) * scale)


def _split_tables(tabs, batch):
    real = {n: t[N_META:] for n, t in tabs.items()}
    meta = {n: jnp.tile(t[:N_META], (batch, 1)) for n, t in tabs.items()}
    return real, meta


def _trunk(x, meta_tokens, layers, tile_rows=512):
    b, seq, d = x.shape
    depth = len(layers)
    alpha = (2.0 * depth) ** 0.25
    length = seq + N_META
    h_r = x.reshape(b * seq, d)
    h_m = jnp.broadcast_to(meta_tokens[None].astype(x.dtype), (b, N_META, d)).reshape(b * N_META, d)
    mla_r, mla_m = _split_tables(_mla_tables(length), b)
    gqa_r, gqa_m = _split_tables(_gqa_tables(length), b)
    tm_r = _row_tile(seq, tile_rows)
    tm_m = _row_tile(b * N_META, tile_rows)
    shp = lambda t, n: t.reshape(b, n, t.shape[-1])
    for i, layer in enumerate(layers):
        last = i == depth - 1
        if layer['kind'] == 'mla':
            q_r, k_r, v_r = _mla_proj(h_r, mla_r, layer['proj'], tm_r)
            q_m, k_m, v_m = _mla_proj(h_m, mla_m, layer['proj'], tm_m)
            a_r, a_m = _mla_attn(shp(q_r, seq), shp(k_r, seq), shp(v_r, seq),
                                 shp(q_m, N_META), shp(k_m, N_META), shp(v_m, N_META))
        else:
            assert last, "windowed layer computes real-token outputs only"
            k_r, v_r, q_r = _gqa_proj(h_r, gqa_r, layer['proj'], tm_r, True)
            k_m, v_m = _gqa_proj(h_m, gqa_m, layer['proj'], tm_m, False)
            a_r = _gqa_attn(layer['sink2'], shp(q_r, seq), shp(k_r, seq), shp(v_r, seq),
                            shp(k_m, N_META), shp(v_m, N_META))
            a_m = None
        h_r = _post(a_r.reshape(b * seq, d), h_r, layer['post'], alpha, _row_tile(b * seq, POST_TILE_ROWS))
        if not last:
            h_m = _post(a_m.reshape(b * N_META, d), h_m, layer['post'], alpha, tm_m)
    return h_r.reshape(b, seq, d)


def kernel(x_prompt, x_sample, meta_tokens, mla_w_in, mla_g_q, mla_w_uq, mla_g_kv, mla_w_ukv, mla_w_o,
           gqa_w_qkv, gqa_sink, gqa_w_o, mlp_w1, mlp_w2, ln1_g, ln1_b, ln2_g, ln2_b):
    depth = mlp_w1.shape[0]
    layers = []
    for i in range(depth):
        j = i // 2
        if i % 2 == 0:
            proj, wo = _prep_mla(mla_w_in[j], mla_g_q[j], mla_w_uq[j], mla_g_kv[j], mla_w_ukv[j], mla_w_o[j])
            layer = dict(kind='mla', proj=proj)
        else:
            proj, wo = _prep_gqa(gqa_w_qkv[j], gqa_w_o[j])
            layer = dict(kind='gqa', proj=proj, sink2=gqa_sink[j].astype(F32) * LOG2E)
        row = lambda t: t[i].reshape(1, -1).astype(F32)
        layer['post'] = dict(wo=wo, g1=row(ln1_g), b1=row(ln1_b), w1=mlp_w1[i].astype(BF16),
                             w2=mlp_w2[i].astype(BF16), g2=row(ln2_g), b2=row(ln2_b))
        layers.append(layer)
    return (_trunk(x_prompt, meta_tokens, layers), _trunk(x_sample, meta_tokens, layers))
```

```python
import functools
import math

import numpy as np
import jax
import jax.numpy as jnp
from jax import lax
from jax.experimental import pallas as pl
from jax.experimental.pallas import tpu as pltpu

D_MODEL = 1024
N_META = 16
BLOCK = 128
WINDOW = 128
ROPE_THETA = 500000.0
MLA_HEADS = 16
MLA_NOPE = 64
MLA_ROPE = 32
MLA_V = 64
MLA_KV_RANK = 256
MLA_Q_RANK = 768
GQA_Q_HEADS = 16
GQA_KV_HEADS = 4
GQA_HEAD_DIM = 64
GQA_ROT = 16
D_FF = 4096
LN_EPS = 1e-5
RMS_EPS = 1e-6
NEG_INF = -1e30
LOG2E = math.log2(math.e)

LANES = 128
VMEM_LIMIT_BYTES = 56 * 1024 * 1024

BF16 = jnp.bfloat16
F32 = jnp.float32


def _row_tile(rows, target):
    if rows <= target:
        return rows
    t = target
    while rows % t:
        t //= 2
    return t


def _const_spec(shape):
    nd = len(shape)
    return pl.BlockSpec(shape, lambda *_: (0,) * nd, pipeline_mode=pl.Buffered(1))


def _nt_dot(a, b):
    return lax.dot_general(a, b, (((1,), (1,)), ((), ())), preferred_element_type=F32)


def _dot(a, b):
    return jnp.dot(a, b, preferred_element_type=F32)


def _mla_proj_kernel(x_ref, tq_ref, ta_ref, tb_ref, w_in_ref, gq_ref, gkv_ref, wq_ref, wkv_ref,
                     q_ref, k_ref, v_ref):
    xb = x_ref[...].astype(BF16)
    c = _dot(xb, w_in_ref[...])
    cq = c[:, :MLA_Q_RANK]
    cq = cq * lax.rsqrt(jnp.mean(cq * cq, axis=-1, keepdims=True) + RMS_EPS) * gq_ref[...]
    kv0 = MLA_Q_RANK
    ckv = c[:, kv0:kv0 + MLA_KV_RANK]
    ckv = ckv * lax.rsqrt(jnp.mean(ckv * ckv, axis=-1, keepdims=True) + RMS_EPS) * gkv_ref[...]
    r0 = kv0 + MLA_KV_RANK
    kr = c[:, r0:r0 + LANES] * ta_ref[...] + c[:, r0 + LANES:r0 + 2 * LANES] * tb_ref[...]
    q = _dot(cq.astype(BF16), wq_ref[...])
    kv = _dot(ckv.astype(BF16), wkv_ref[...])
    tq = tq_ref[...]
    first = lax.broadcasted_iota(jnp.int32, (1, LANES), 1) < MLA_NOPE
    ones = jnp.ones((kv.shape[0], LANES), BF16)
    for h in range(MLA_HEADS):
        sl = slice(h * LANES, (h + 1) * LANES)
        q_ref[:, sl] = (q[:, sl] * tq).astype(BF16)
        k_ref[:, sl] = jnp.where(first, kv[:, sl], kr).astype(BF16)
    for j in range(MLA_HEADS // 2):
        v_even = pltpu.roll(kv[:, (2 * j) * LANES:(2 * j + 1) * LANES], MLA_V, 1)
        v_odd = kv[:, (2 * j + 1) * LANES:(2 * j + 2) * LANES]
        v_ref[:, (2 * j) * LANES:(2 * j + 1) * LANES] = jnp.where(first, v_even, v_odd).astype(BF16)
        v_ref[:, (2 * j + 1) * LANES:(2 * j + 2) * LANES] = ones


def _mla_proj(x2d, tabs, w, tm, nper):
    rows = x2d.shape[0]
    row_spec = lambda n: pl.BlockSpec((tm, n), lambda i: (i, 0))
    tab_spec = pl.BlockSpec((tm, LANES), lambda i: (i % nper, 0))
    hq = MLA_HEADS * LANES
    return pl.pallas_call(
        _mla_proj_kernel,
        grid=(rows // tm,),
        in_specs=[row_spec(D_MODEL), tab_spec, tab_spec, tab_spec,
                  _const_spec(w['w_in'].shape), _const_spec(w['g_q'].shape), _const_spec(w['g_kv'].shape),
                  _const_spec(w['wq'].shape), _const_spec(w['wkv'].shape)],
        out_specs=[row_spec(hq), row_spec(hq), row_spec(hq)],
        out_shape=[jax.ShapeDtypeStruct((rows, hq), BF16)] * 3,
        compiler_params=pltpu.CompilerParams(dimension_semantics=("parallel",),
                                             vmem_limit_bytes=VMEM_LIMIT_BYTES),
        name="mla_proj",
    )(x2d, tabs['tq'], tabs['ta'], tabs['tb'], w['w_in'], w['g_q'], w['g_kv'], w['wq'], w['wkv'])


MLA_Q_ROWS = 1024
MLA_KEY_BLOCK = 2048

def _mla_attn_kernel(qr_ref, kr_ref, vr_ref, qm_ref, km_ref, vm_ref, or_ref, om_ref,
                     vxm_ref, kms_ref, qmx_ref, acc_ref, m_ref, *, tq, kb):
    seq = qr_ref.shape[1]
    pair = 2 * LANES
    nkb = seq // kb
    vxm_ref[...] = jnp.zeros(vxm_ref.shape, BF16)
    vxm_ref[:N_META, :] = vm_ref[0]
    kms_ref[...] = jnp.zeros(kms_ref.shape, BF16)
    kms_ref[:N_META, :] = km_ref[0]
    qmx_ref[...] = jnp.zeros(qmx_ref.shape, BF16)
    for t in range(2):
        qmx_ref[t * N_META:(t + 1) * N_META, t * LANES:(t + 1) * LANES] = qm_ref[0, :, t * LANES:(t + 1) * LANES]
    lane = lax.broadcasted_iota(jnp.int32, (1, LANES), 1)
    meta_valid = lane < N_META
    first_head = lane < MLA_V

    def attend(q, feat):
        parts = []
        for j in range(nkb):
            rows = slice(j * kb, (j + 1) * kb)
            s = _nt_dot(q, kr_ref[0, rows, feat])
            m = jnp.max(s, axis=-1, keepdims=True)
            if j == nkb - 1:
                s_m = jnp.where(meta_valid, _nt_dot(q, kms_ref[:, feat]), NEG_INF)
                m = jnp.maximum(m, jnp.max(s_m, axis=-1, keepdims=True))
            o = _dot(jnp.exp2(s - m).astype(BF16), vr_ref[0, rows, :])
            if j == nkb - 1:
                o = o + _dot(jnp.exp2(s_m - m).astype(BF16), vxm_ref[...])
            parts.append((m, o))
        m_all, o = parts[0]
        if nkb > 1:
            for m_j, _ in parts[1:]:
                m_all = jnp.maximum(m_all, m_j)
            o = sum(o_j * jnp.exp2(m_j - m_all) for m_j, o_j in parts)
        return o[:, :LANES] / o[:, LANES:]

    def chunk(ci, carry):
        r0 = pl.multiple_of(ci * tq, tq)
        res = [attend(qr_ref[0, pl.ds(r0, tq), t * LANES:(t + 1) * LANES], slice(t * LANES, (t + 1) * LANES))
               for t in range(2)]
        or_ref[0, pl.ds(r0, tq), :] = jnp.where(first_head, res[0], res[1]).astype(BF16)
        return carry

    def chunk_block(it, carry):
        ci, j = it // nkb, it % nkb
        r0 = pl.multiple_of(ci * tq, tq)
        k0 = pl.multiple_of(j * kb, kb)
        first_block = j == 0
        with_meta = jnp.where(j == nkb - 1, 1.0, 0.0)
        res = []
        for t in range(2):
            feat = slice(t * LANES, (t + 1) * LANES)
            q = qr_ref[0, pl.ds(r0, tq), feat]
            s = _nt_dot(q, kr_ref[0, pl.ds(k0, kb), feat])
            s_m = jnp.where(meta_valid, _nt_dot(q, kms_ref[:, feat]), NEG_INF)
            m_prev = jnp.where(first_block, NEG_INF, m_ref[t])
            m = jnp.maximum(jnp.maximum(m_prev, jnp.max(s, axis=-1, keepdims=True)),
                            jnp.max(s_m, axis=-1, keepdims=True))
            o = (acc_ref[t] * jnp.exp2(m_prev - m) + _dot(jnp.exp2(s - m).astype(BF16), vr_ref[0, pl.ds(k0, kb), :]) +
                 _dot((jnp.exp2(s_m - m) * with_meta).astype(BF16), vxm_ref[...]))
            acc_ref[t] = o
            m_ref[t] = m
            res.append(o[:, :LANES] / o[:, LANES:])
        or_ref[0, pl.ds(r0, tq), :] = jnp.where(first_head, res[0], res[1]).astype(BF16)
        return carry

    if nkb == 1:
        lax.fori_loop(0, seq // tq, chunk, 0)
    else:
        acc_ref[...] = jnp.zeros(acc_ref.shape, F32)
        lax.fori_loop(0, (seq // tq) * nkb, chunk_block, 0)
    res = attend(qmx_ref[...], slice(0, pair))
    om_ref[0] = jnp.where(first_head, res[:N_META], res[N_META:]).astype(BF16)


def _mla_attn(q_r, k_r, v_r, q_m, k_m, v_m):
    b, seq, _ = q_r.shape
    kb = _row_tile(seq, MLA_KEY_BLOCK)
    tq = _row_tile(seq, MLA_Q_ROWS if kb == seq else MLA_Q_ROWS // 2)
    pair = 2 * LANES
    real = lambda n: pl.BlockSpec((1, seq, n), lambda i, j: (i, 0, j))
    meta = lambda n: pl.BlockSpec((1, N_META, n), lambda i, j: (i, 0, j))
    return pl.pallas_call(
        functools.partial(_mla_attn_kernel, tq=tq, kb=kb),
        grid=(b, MLA_HEADS // 2),
        in_specs=[real(pair), real(pair), real(pair), meta(pair), meta(pair), meta(pair)],
        out_specs=[real(LANES), meta(LANES)],
        out_shape=[jax.ShapeDtypeStruct((b, seq, MLA_HEADS * MLA_V), BF16),
                   jax.ShapeDtypeStruct((b, N_META, MLA_HEADS * MLA_V), BF16)],
        scratch_shapes=[pltpu.VMEM((LANES, pair), BF16),
                        pltpu.VMEM((LANES, pair), BF16), pltpu.VMEM((2 * N_META, pair), BF16),
                        pltpu.VMEM((2, tq, pair), F32), pltpu.VMEM((2, tq, 1), F32)],
        compiler_params=pltpu.CompilerParams(dimension_semantics=("parallel", "parallel"),
                                             vmem_limit_bytes=VMEM_LIMIT_BYTES),
        name="mla_attn",
    )(q_r, k_r, v_r, q_m, k_m, v_m)


POST_TILE_ROWS = 512
POST_SUB_ROWS = 256
POST_FF_CHUNK = 1024


def _layer_norm(x, g, b):
    mu = jnp.mean(x, axis=-1, keepdims=True)
    xc = x - mu
    var = jnp.mean(xc * xc, axis=-1, keepdims=True)
    return xc * lax.rsqrt(var + LN_EPS) * g + b


def _post_kernel(a_ref, h_ref, wo_ref, g1_ref, b1_ref, w1_ref, w2_ref, g2_ref, b2_ref, o_ref, *,
                 alpha, ff_chunk, sub_rows):
    subs = [slice(r * sub_rows, (r + 1) * sub_rows) for r in range(a_ref.shape[0] // sub_rows)]
    mixes = [_dot(a_ref[rows, :], wo_ref[...]) for rows in subs]
    pending = None
    for rows, mix in zip(subs, mixes):
        h1 = _layer_norm(alpha * h_ref[rows, :] + mix, g1_ref[...], b1_ref[...])
        h1b = h1.astype(BF16)
        acc = jnp.zeros(h1.shape, F32)
        for c in range(D_FF // ff_chunk):
            sl = slice(c * ff_chunk, (c + 1) * ff_chunk)
            u = jnp.maximum(_dot(h1b, w1_ref[:, sl]), 0.0)
            acc = acc + _dot((u * u).astype(BF16), w2_ref[sl, :])
        if pending is not None:
            p_rows, p_h1, p_acc = pending
            o_ref[p_rows, :] = _layer_norm(alpha * p_h1 + p_acc, g2_ref[...], b2_ref[...])
        pending = (rows, h1, acc)
    p_rows, p_h1, p_acc = pending
    o_ref[p_rows, :] = _layer_norm(alpha * p_h1 + p_acc, g2_ref[...], b2_ref[...])


def _post(a2d, h2d, w, alpha, tm):
    rows = a2d.shape[0]
    row_spec = pl.BlockSpec((tm, D_MODEL), lambda i: (i, 0))
    names = ('wo', 'g1', 'b1', 'w1', 'w2', 'g2', 'b2')
    return pl.pallas_call(
        functools.partial(_post_kernel, alpha=alpha, ff_chunk=POST_FF_CHUNK, sub_rows=min(tm, POST_SUB_ROWS)),
        grid=(rows // tm,),
        in_specs=[row_spec, row_spec] + [_const_spec(w[n].shape) for n in names],
        out_specs=row_spec,
        out_shape=jax.ShapeDtypeStruct((rows, D_MODEL), F32),
        compiler_params=pltpu.CompilerParams(dimension_semantics=("parallel",),
                                             vmem_limit_bytes=VMEM_LIMIT_BYTES),
        name="post_mlp",
    )(a2d, h2d, *[w[n] for n in names])


def _rope_block(blk, c, s1, s2):
    return blk * c + pltpu.roll(blk, LANES - GQA_ROT // 2, 1) * s1 + pltpu.roll(blk, GQA_ROT // 2, 1) * s2


def _gqa_proj_kernel(*refs, with_q):
    if with_q:
        h_ref, kc_ref, ks1_ref, ks2_ref, wk_ref, wv_ref, qc_ref, qs1_ref, qs2_ref, wq_ref, k_ref, v_ref, q_ref = refs
    else:
        h_ref, kc_ref, ks1_ref, ks2_ref, wk_ref, wv_ref, k_ref, v_ref = refs
    hb = h_ref[...].astype(BF16)
    if with_q:
        q = _dot(hb, wq_ref[...])
        qc, qs1, qs2 = qc_ref[...], qs1_ref[...], qs2_ref[...]
        for j in range(q.shape[1] // LANES):
            sl = slice(j * LANES, (j + 1) * LANES)
            q_ref[:, sl] = _rope_block(q[:, sl], qc, qs1, qs2).astype(BF16)
    k = _dot(hb, wk_ref[...])
    v = _dot(hb, wv_ref[...])
    kc, ks1, ks2 = kc_ref[...], ks1_ref[...], ks2_ref[...]
    first = lax.broadcasted_iota(jnp.int32, (1, LANES), 1) < GQA_HEAD_DIM
    ones = jnp.ones((k.shape[0], LANES), BF16)
    for j in range(GQA_KV_HEADS // 2):
        sl = slice(j * LANES, (j + 1) * LANES)
        kb = _rope_block(k[:, sl], kc, ks1, ks2)
        vb = v[:, sl]
        ksw = pltpu.roll(kb, GQA_HEAD_DIM, 1)
        vsw = pltpu.roll(vb, GQA_HEAD_DIM, 1)
        for g, (kx, ky, vx, vy) in ((2 * j, (kb, ksw, vb, vsw)), (2 * j + 1, (ksw, kb, vsw, vb))):
            k_ref[:, g * LANES:(g + 1) * LANES] = jnp.where(first, kx, ky).astype(BF16)
            v_ref[:, (2 * g) * LANES:(2 * g + 1) * LANES] = jnp.where(first, vx, vy).astype(BF16)
            v_ref[:, (2 * g + 1) * LANES:(2 * g + 2) * LANES] = ones


def _gqa_proj(h2d, tabs, w, tm, nper, with_q):
    rows = h2d.shape[0]
    row_spec = lambda n: pl.BlockSpec((tm, n), lambda i: (i, 0))
    tab_spec = pl.BlockSpec((tm, LANES), lambda i: (i % nper, 0))
    nq = GQA_Q_HEADS * GQA_HEAD_DIM
    nk = GQA_KV_HEADS * LANES
    nv = GQA_KV_HEADS * 2 * LANES
    args = [h2d, tabs['kc'], tabs['ks1'], tabs['ks2'], w['wk'], w['wv']]
    in_specs = [row_spec(D_MODEL)] + [tab_spec] * 3 + [_const_spec(w['wk'].shape), _const_spec(w['wv'].shape)]
    out_specs = [row_spec(nk), row_spec(nv)]
    out_shape = [jax.ShapeDtypeStruct((rows, nk), BF16), jax.ShapeDtypeStruct((rows, nv), BF16)]
    if with_q:
        args += [tabs['qc'], tabs['qs1'], tabs['qs2'], w['wq']]
        in_specs += [tab_spec] * 3 + [_const_spec(w['wq'].shape)]
        out_specs.append(row_spec(nq))
        out_shape.append(jax.ShapeDtypeStruct((rows, nq), BF16))
    return pl.pallas_call(
        functools.partial(_gqa_proj_kernel, with_q=with_q),
        grid=(rows // tm,),
        in_specs=in_specs,
        out_specs=out_specs,
        out_shape=out_shape,
        compiler_params=pltpu.CompilerParams(dimension_semantics=("parallel",),
                                             vmem_limit_bytes=VMEM_LIMIT_BYTES),
        name="gqa_proj",
    )(*args)


def _gqa_attn_kernel(sink_ref, q_ref, kp_ref, kc_ref, kn_ref, vp_ref, vc_ref, vn_ref, km_ref, vm_ref, o_ref,
                     kw_ref, vw_ref, *, tq):
    i = pl.program_id(1)
    nsub = tq // BLOCK
    nblk = pl.num_programs(1) * nsub
    mrow = (nsub - 1) * BLOCK
    for src, dst in ((km_ref, kw_ref), (vm_ref, vw_ref)):
        dst[mrow:mrow + BLOCK, :] = jnp.zeros((BLOCK, dst.shape[1]), BF16)
        dst[mrow:mrow + N_META, :] = src[0]
    for prv, cur, nxt, dst in ((kp_ref, kc_ref, kn_ref, kw_ref), (vp_ref, vc_ref, vn_ref, vw_ref)):
        dst[0:BLOCK, :] = prv[0]
        dst[BLOCK:mrow, :] = cur[0, 0:mrow - BLOCK, :]
        dst[mrow + BLOCK:tq + 2 * BLOCK, :] = cur[0, mrow - BLOCK:tq, :]
        dst[tq + 2 * BLOCK:tq + 3 * BLOCK, :] = nxt[0]

    span = 4 * BLOCK
    group = GQA_Q_HEADS // GQA_KV_HEADS
    qi = lax.broadcasted_iota(jnp.int32, (BLOCK, 1), 0)
    col = lax.broadcasted_iota(jnp.int32, (1, span), 1)
    col_blk = col // BLOCK
    col_in = col % BLOCK
    first_head = lax.broadcasted_iota(jnp.int32, (1, LANES), 1) < GQA_HEAD_DIM
    head_idx = lax.broadcasted_iota(jnp.int32, (group, 1, 1), 0)
    always = 1 << 30

    def sub_block(s, carry):
        q0 = pl.multiple_of(s * BLOCK, BLOCK)
        blk = i * nsub + s
        mpos = nsub - 1 - s
        is_meta = col_blk == mpos
        rel = col_blk - jnp.where(col_blk > mpos, 1, 0)
        dead = ((rel == 0) & (blk == 0)) | ((rel == 2) & (blk == nblk - 1))
        centre = jnp.where(is_meta, 0, (rel - 1) * BLOCK + col_in)
        reach = jnp.where(is_meta, jnp.where(col_in < N_META, always, -1), jnp.where(dead, -1, WINDOW))
        ceiling = jnp.where(jnp.abs(centre - qi) <= reach, jnp.inf, NEG_INF)[None]
        for g in range(GQA_KV_HEADS):
            lhs = []
            for pr in range(group // 2):
                qp = q_ref[0, pl.ds(q0, BLOCK), (2 * g + pr) * LANES:(2 * g + pr + 1) * LANES]
                lhs += [jnp.where(first_head, qp, jnp.zeros_like(qp)), jnp.where(first_head, jnp.zeros_like(qp), qp)]
            lhs = jnp.concatenate(lhs, axis=0)
            sc = _nt_dot(lhs, kw_ref[pl.ds(q0, span), g * LANES:(g + 1) * LANES])
            sc = jnp.minimum(sc.reshape(group, BLOCK, span), ceiling)
            sink = sink_ref[group * g + group - 1]
            for hh in range(group - 2, -1, -1):
                sink = jnp.where(head_idx == hh, sink_ref[group * g + hh], sink)
            m = jnp.maximum(jnp.max(sc, axis=-1, keepdims=True), sink)
            p = jnp.exp2(sc - m).astype(BF16).reshape(group * BLOCK, span)
            o = _dot(p, vw_ref[pl.ds(q0, span), 2 * g * LANES:2 * (g + 1) * LANES])
            o = o.reshape(group, BLOCK, 2 * LANES)
            res = o[:, :, :LANES] / (o[:, :, LANES:] + jnp.exp2(sink - m))
            for pr in range(group // 2):
                o_ref[0, pl.ds(q0, BLOCK), (2 * g + pr) * LANES:(2 * g + pr + 1) * LANES] = jnp.where(
                    first_head, res[2 * pr], res[2 * pr + 1]).astype(BF16)
        return carry

    lax.fori_loop(0, nsub, sub_block, 0)


def _gqa_attn(sink2, q, k, v, k_m, v_m):
    b, seq, nq = q.shape
    nk, nv = k.shape[2], v.shape[2]
    tq = 4 * BLOCK
    assert seq % tq == 0
    nsub = tq // BLOCK
    nblk = seq // BLOCK
    tile = lambda n: pl.BlockSpec((1, tq, n), lambda bi, i: (bi, i, 0))
    prev = lambda n: pl.BlockSpec((1, BLOCK, n), lambda bi, i: (bi, jnp.maximum(i * nsub - 1, 0), 0))
    nxt = lambda n: pl.BlockSpec((1, BLOCK, n), lambda bi, i: (bi, jnp.minimum((i + 1) * nsub, nblk - 1), 0))
    meta = lambda n: pl.BlockSpec((1, N_META, n), lambda bi, i: (bi, 0, 0))
    return pl.pallas_call(
        functools.partial(_gqa_attn_kernel, tq=tq),
        grid=(b, seq // tq),
        in_specs=[pl.BlockSpec(memory_space=pltpu.SMEM), tile(nq), prev(nk), tile(nk), nxt(nk),
                  prev(nv), tile(nv), nxt(nv), meta(nk), meta(nv)],
        out_specs=tile(nq),
        out_shape=jax.ShapeDtypeStruct((b, seq, nq), BF16),
        scratch_shapes=[pltpu.VMEM((tq + 3 * BLOCK, nk), BF16), pltpu.VMEM((tq + 3 * BLOCK, nv), BF16)],
        compiler_params=pltpu.CompilerParams(dimension_semantics=("parallel", "parallel"),
                                             vmem_limit_bytes=VMEM_LIMIT_BYTES),
        name="gqa_attn",
    )(sink2, q, k, k, k, v, v, v, k_m, v_m)


def _swap_halves(w):
    half = w.shape[-1] // 2
    return jnp.concatenate([-w[..., half:], w[..., :half]], axis=-1)


def _prep_mla(w_in, g_q, w_uq, g_kv, w_ukv, w_o):
    d = w_in.shape[0]
    z64 = jnp.zeros((d, MLA_NOPE), F32)
    wkr = w_in[:, MLA_Q_RANK + MLA_KV_RANK:]
    wkr_sw = _swap_halves(wkr)
    w_in_x = jnp.concatenate([w_in[:, :MLA_Q_RANK + MLA_KV_RANK], z64, wkr, wkr, z64, wkr_sw, wkr_sw], axis=1)
    wq = w_uq.reshape(MLA_Q_RANK, MLA_HEADS, MLA_NOPE + MLA_ROPE)
    wq = jnp.concatenate([wq, _swap_halves(wq[..., MLA_NOPE:])], axis=-1).reshape(MLA_Q_RANK, MLA_HEADS * LANES)
    assert MLA_NOPE + MLA_V == LANES
    return dict(w_in=w_in_x.astype(BF16), g_q=g_q.reshape(1, -1), g_kv=g_kv.reshape(1, -1),
                wq=wq.astype(BF16), wkv=w_ukv.astype(BF16)), w_o.astype(BF16)


def _prep_gqa(w_qkv, w_o):
    qd = GQA_Q_HEADS * GQA_HEAD_DIM
    kd = GQA_KV_HEADS * GQA_HEAD_DIM
    return dict(wq=w_qkv[:, :qd].astype(BF16), wk=w_qkv[:, qd:qd + kd].astype(BF16),
                wv=w_qkv[:, qd + kd:].astype(BF16)), w_o.astype(BF16)


def _lane_tables(length, dim, lanes, scale=1.0):
    freq, wc, ws, const = (np.asarray(col) for col in zip(*lanes))
    pos = jnp.arange(length, dtype=F32)
    inv = ROPE_THETA ** (-jnp.arange(0, dim, 2, dtype=F32) / dim)
    ang = pos[:, None] * inv[freq][None, :]
    table = jnp.cos(ang) * wc.astype(np.float32) + jnp.sin(ang) * ws.astype(np.float32) + const.astype(np.float32)
    return table * scale if scale != 1.0 else table


def _mla_tables(length):
    half = MLA_ROPE // 2
    nope = [(0, 0, 0, 0)] * MLA_NOPE
    cos = [(i % half, 1, 0, 0) for i in range(MLA_ROPE)]
    sin = [(i % half, 0, 1, 0) for i in range(MLA_ROPE)]
    scale = (MLA_NOPE + MLA_ROPE) ** -0.5 * LOG2E
    return dict(tq=_lane_tables(length, MLA_ROPE, [(0, 0, 0, 1)] * MLA_NOPE + cos + sin, scale),
                ta=_lane_tables(length, MLA_ROPE, nope + cos + cos),
                tb=_lane_tables(length, MLA_ROPE, nope + sin + sin))


def _gqa_tables(length):
    half = GQA_ROT // 2
    rest = GQA_HEAD_DIM - GQA_ROT
    two = lambda head: head * (LANES // GQA_HEAD_DIM)
    c = two([(i % half, 1, 0, 0) for i in range(GQA_ROT)] + [(0, 0, 0, 1)] * rest)
    s1 = two([(i, 0, -1, 0) for i in range(half)] + [(0, 0, 0, 0)] * (half + rest))
    s2 = two([(0, 0, 0, 0)] * half + [(i, 0, 1, 0) for i in range(half)] + [(0, 0, 0, 0)] * rest)
    scale = GQA_HEAD_DIM ** -0.5 * LOG2E
    tab = lambda lanes, sc=1.0: _lane_tables(length, GQA_ROT, lanes, sc)
    return dict(kc=tab(c), ks1=tab(s1), ks2=tab(s2), qc=tab(c, scale), qs1=tab(s1, scale), qs2=tab(s2, scale))


def _split_tables(tabs):
    return {n: t[N_META:] for n, t in tabs.items()}, {n: t[:N_META] for n, t in tabs.items()}


def _trunk(x, meta_tokens, layers, tables, tile_rows=1024):
    b, seq, d = x.shape
    depth = len(layers)
    alpha = (2.0 * depth) ** 0.25
    h_r = x.reshape(b * seq, d)
    h_m = jnp.broadcast_to(meta_tokens[None].astype(x.dtype), (b, N_META, d)).reshape(b * N_META, d)
    tm_r = _row_tile(seq, tile_rows)
    tm_m = _row_tile(b * N_META, tile_rows)
    np_r = seq // tm_r
    np_m = b * N_META // tm_m
    (mla_r, mla_m), (gqa_r, gqa_m) = (_split_tables(t) for t in tables)
    mla_m, gqa_m = ({n: jnp.tile(t, (b, 1)) for n, t in tm.items()} for tm in (mla_m, gqa_m))
    shp = lambda t, n: t.reshape(b, n, t.shape[-1])
    for i, layer in enumerate(layers):
        last = i == depth - 1
        if layer['kind'] == 'mla':
            q_r, k_r, v_r = _mla_proj(h_r, mla_r, layer['proj'], tm_r, np_r)
            q_m, k_m, v_m = _mla_proj(h_m, mla_m, layer['proj'], tm_m, np_m)
            a_r, a_m = _mla_attn(shp(q_r, seq), shp(k_r, seq), shp(v_r, seq),
                                 shp(q_m, N_META), shp(k_m, N_META), shp(v_m, N_META))
        else:
            assert last, "windowed layer computes real-token outputs only"
            k_r, v_r, q_r = _gqa_proj(h_r, gqa_r, layer['proj'], tm_r, np_r, True)
            k_m, v_m = _gqa_proj(h_m, gqa_m, layer['proj'], tm_m, np_m, False)
            a_r = _gqa_attn(layer['sink2'], shp(q_r, seq), shp(k_r, seq), shp(v_r, seq),
                            shp(k_m, N_META), shp(v_m, N_META))
            a_m = None
        h_r = _post(a_r.reshape(b * seq, d), h_r, layer['post'], alpha, _row_tile(b * seq, POST_TILE_ROWS))
        if not last:
            h_m = _post(a_m.reshape(b * N_META, d), h_m, layer['post'], alpha, tm_m)
    return h_r.reshape(b, seq, d)


def kernel(x_prompt, x_sample, meta_tokens, mla_w_in, mla_g_q, mla_w_uq, mla_g_kv, mla_w_ukv, mla_w_o,
           gqa_w_qkv, gqa_sink, gqa_w_o, mlp_w1, mlp_w2, ln1_g, ln1_b, ln2_g, ln2_b):
    depth = mlp_w1.shape[0]
    layers = []
    for i in range(depth):
        j = i // 2
        if i % 2 == 0:
            proj, wo = _prep_mla(mla_w_in[j], mla_g_q[j], mla_w_uq[j], mla_g_kv[j], mla_w_ukv[j], mla_w_o[j])
            layer = dict(kind='mla', proj=proj)
        else:
            proj, wo = _prep_gqa(gqa_w_qkv[j], gqa_w_o[j])
            layer = dict(kind='gqa', proj=proj, sink2=gqa_sink[j].astype(F32) * LOG2E)
        row = lambda t: t[i].reshape(1, -1).astype(F32)
        layer['post'] = dict(wo=wo, g1=row(ln1_g), b1=row(ln1_b), w1=mlp_w1[i].astype(BF16),
                             w2=mlp_w2[i].astype(BF16), g2=row(ln2_g), b2=row(ln2_b))
        layers.append(layer)
    length = max(x_prompt.shape[1], x_sample.shape[1]) + N_META
    tables = (_mla_tables(length), _gqa_tables(length))
    return (_trunk(x_prompt, meta_tokens, layers, tables), _trunk(x_sample, meta_tokens, layers, tables))
```

```python
import functools
import math

import numpy as np
import jax
import jax.numpy as jnp
from jax import lax
from jax.experimental import pallas as pl
from jax.experimental.pallas import tpu as pltpu

D_MODEL = 1024
N_META = 16
BLOCK = 128
WINDOW = 128
ROPE_THETA = 500000.0
MLA_HEADS = 16
MLA_NOPE = 64
MLA_ROPE = 32
MLA_V = 64
MLA_KV_RANK = 256
MLA_Q_RANK = 768
GQA_Q_HEADS = 16
GQA_KV_HEADS = 4
GQA_HEAD_DIM = 64
GQA_ROT = 16
D_FF = 4096
LN_EPS = 1e-5
RMS_EPS = 1e-6
NEG_INF = -1e30
LOG2E = math.log2(math.e)

LANES = 128
VMEM_LIMIT_BYTES = 56 * 1024 * 1024

BF16 = jnp.bfloat16
F32 = jnp.float32


def _row_tile(rows, target):
    if rows <= target:
        return rows
    t = target
    while rows % t:
        t //= 2
    return t


def _const_spec(shape):
    nd = len(shape)
    return pl.BlockSpec(shape, lambda *_: (0,) * nd, pipeline_mode=pl.Buffered(1))


def _nt_dot(a, b):
    return lax.dot_general(a, b, (((1,), (1,)), ((), ())), preferred_element_type=F32)


def _dot(a, b):
    return jnp.dot(a, b, preferred_element_type=F32)


def _mla_proj_kernel(x_ref, tq_ref, ta_ref, tb_ref, w_in_ref, gq_ref, gkv_ref, wq_ref, wkv_ref,
                     q_ref, k_ref, v_ref):
    xb = x_ref[...].astype(BF16)
    c = _dot(xb, w_in_ref[...])
    cq = c[:, :MLA_Q_RANK]
    cq = cq * lax.rsqrt(jnp.mean(cq * cq, axis=-1, keepdims=True) + RMS_EPS) * gq_ref[...]
    kv0 = MLA_Q_RANK
    ckv = c[:, kv0:kv0 + MLA_KV_RANK]
    ckv = ckv * lax.rsqrt(jnp.mean(ckv * ckv, axis=-1, keepdims=True) + RMS_EPS) * gkv_ref[...]
    r0 = kv0 + MLA_KV_RANK
    kr = c[:, r0:r0 + LANES] * ta_ref[...] + c[:, r0 + LANES:r0 + 2 * LANES] * tb_ref[...]
    q = _dot(cq.astype(BF16), wq_ref[...])
    kv = _dot(ckv.astype(BF16), wkv_ref[...])
    tq = tq_ref[...]
    first = lax.broadcasted_iota(jnp.int32, (1, LANES), 1) < MLA_NOPE
    ones = jnp.ones((kv.shape[0], LANES), BF16)
    for h in range(MLA_HEADS):
        sl = slice(h * LANES, (h + 1) * LANES)
        q_ref[:, sl] = (q[:, sl] * tq).astype(BF16)
        k_ref[:, sl] = jnp.where(first, kv[:, sl], kr).astype(BF16)
    for j in range(MLA_HEADS // 2):
        v_even = pltpu.roll(kv[:, (2 * j) * LANES:(2 * j + 1) * LANES], MLA_V, 1)
        v_odd = kv[:, (2 * j + 1) * LANES:(2 * j + 2) * LANES]
        v_ref[:, (2 * j) * LANES:(2 * j + 1) * LANES] = jnp.where(first, v_even, v_odd).astype(BF16)
        v_ref[:, (2 * j + 1) * LANES:(2 * j + 2) * LANES] = ones


def _mla_proj(x2d, tabs, w, tm, nper):
    rows = x2d.shape[0]
    row_spec = lambda n: pl.BlockSpec((tm, n), lambda i: (i, 0))
    tab_spec = pl.BlockSpec((tm, LANES), lambda i: (i % nper, 0))
    hq = MLA_HEADS * LANES
    return pl.pallas_call(
        _mla_proj_kernel,
        grid=(rows // tm,),
        in_specs=[row_spec(D_MODEL), tab_spec, tab_spec, tab_spec,
                  _const_spec(w['w_in'].shape), _const_spec(w['g_q'].shape), _const_spec(w['g_kv'].shape),
                  _const_spec(w['wq'].shape), _const_spec(w['wkv'].shape)],
        out_specs=[row_spec(hq), row_spec(hq), row_spec(hq)],
        out_shape=[jax.ShapeDtypeStruct((rows, hq), BF16)] * 3,
        compiler_params=pltpu.CompilerParams(dimension_semantics=("parallel",),
                                             vmem_limit_bytes=VMEM_LIMIT_BYTES),
        name="mla_proj",
    )(x2d, tabs['tq'], tabs['ta'], tabs['tb'], w['w_in'], w['g_q'], w['g_kv'], w['wq'], w['wkv'])


MLA_Q_ROWS = 1024
MLA_KEY_BLOCK = 2048


MLA_PAIRS_PER_STEP = 2


def _mla_attn_kernel(qr_ref, kr_ref, vr_ref, qm_ref, km_ref, vm_ref, or_ref, om_ref,
                     vxm_ref, kms_ref, qmx_ref, acc_ref, m_ref, *, tq, kb):
    pair = 2 * LANES
    for pp in range(MLA_PAIRS_PER_STEP):
        wide = lambda ref: ref.at[:, :, pl.ds(pp * pair, pair)]
        narrow = lambda ref: ref.at[:, :, pl.ds(pp * LANES, LANES)]
        _mla_pair_body(wide(qr_ref), wide(kr_ref), wide(vr_ref), wide(qm_ref), wide(km_ref), wide(vm_ref),
                       narrow(or_ref), narrow(om_ref), vxm_ref.at[pp], kms_ref.at[pp], qmx_ref.at[pp],
                       acc_ref, m_ref, tq=tq, kb=kb)


def _mla_pair_body(qr_ref, kr_ref, vr_ref, qm_ref, km_ref, vm_ref, or_ref, om_ref,
                   vxm_ref, kms_ref, qmx_ref, acc_ref, m_ref, *, tq, kb):
    seq = qr_ref.shape[1]
    pair = 2 * LANES
    nkb = seq // kb
    vxm_ref[...] = jnp.zeros(vxm_ref.shape, BF16)
    vxm_ref[:N_META, :] = vm_ref[0]
    kms_ref[...] = jnp.zeros(kms_ref.shape, BF16)
    kms_ref[:N_META, :] = km_ref[0]
    qmx_ref[...] = jnp.zeros(qmx_ref.shape, BF16)
    for t in range(2):
        qmx_ref[t * N_META:(t + 1) * N_META, t * LANES:(t + 1) * LANES] = qm_ref[0, :, t * LANES:(t + 1) * LANES]
    lane = lax.broadcasted_iota(jnp.int32, (1, LANES), 1)
    meta_valid = lane < N_META
    first_head = lane < MLA_V

    def attend(q, feat):
        parts = []
        for j in range(nkb):
            rows = slice(j * kb, (j + 1) * kb)
            s = _nt_dot(q, kr_ref[0, rows, feat])
            m = jnp.max(s, axis=-1, keepdims=True)
            if j == nkb - 1:
                s_m = jnp.where(meta_valid, _nt_dot(q, kms_ref[:, feat]), NEG_INF)
                m = jnp.maximum(m, jnp.max(s_m, axis=-1, keepdims=True))
            o = _dot(jnp.exp2(s - m).astype(BF16), vr_ref[0, rows, :])
            if j == nkb - 1:
                o = o + _dot(jnp.exp2(s_m - m).astype(BF16), vxm_ref[...])
            parts.append((m, o))
        m_all, o = parts[0]
        if nkb > 1:
            for m_j, _ in parts[1:]:
                m_all = jnp.maximum(m_all, m_j)
            o = sum(o_j * jnp.exp2(m_j - m_all) for m_j, o_j in parts)
        return o[:, :LANES] / o[:, LANES:]

    def chunk(r0):
        res = [attend(qr_ref[0, r0:r0 + tq, t * LANES:(t + 1) * LANES], slice(t * LANES, (t + 1) * LANES))
               for t in range(2)]
        or_ref[0, r0:r0 + tq, :] = jnp.where(first_head, res[0], res[1]).astype(BF16)

    def chunk_block(it, carry):
        ci, j = it // nkb, it % nkb
        r0 = pl.multiple_of(ci * tq, tq)
        k0 = pl.multiple_of(j * kb, kb)
        first_block = j == 0
        with_meta = jnp.where(j == nkb - 1, 1.0, 0.0)
        res = []
        for t in range(2):
            feat = slice(t * LANES, (t + 1) * LANES)
            q = qr_ref[0, pl.ds(r0, tq), feat]
            s = _nt_dot(q, kr_ref[0, pl.ds(k0, kb), feat])
            s_m = jnp.where(meta_valid, _nt_dot(q, kms_ref[:, feat]), NEG_INF)
            m_prev = jnp.where(first_block, NEG_INF, m_ref[t])
            m = jnp.maximum(jnp.maximum(m_prev, jnp.max(s, axis=-1, keepdims=True)),
                            jnp.max(s_m, axis=-1, keepdims=True))
            o = (acc_ref[t] * jnp.exp2(m_prev - m) + _dot(jnp.exp2(s - m).astype(BF16), vr_ref[0, pl.ds(k0, kb), :]) +
                 _dot((jnp.exp2(s_m - m) * with_meta).astype(BF16), vxm_ref[...]))
            acc_ref[t] = o
            m_ref[t] = m
            res.append(o[:, :LANES] / o[:, LANES:])
        or_ref[0, pl.ds(r0, tq), :] = jnp.where(first_head, res[0], res[1]).astype(BF16)
        return carry

    if nkb == 1:
        for r0 in range(0, seq, tq):
            chunk(r0)
    else:
        acc_ref[...] = jnp.zeros(acc_ref.shape, F32)
        m_ref[...] = jnp.full(m_ref.shape, NEG_INF, F32)
        lax.fori_loop(0, (seq // tq) * nkb, chunk_block, 0)
    res = attend(qmx_ref[...], slice(0, pair))
    om_ref[0] = jnp.where(first_head, res[:N_META], res[N_META:]).astype(BF16)


def _mla_attn(q_r, k_r, v_r, q_m, k_m, v_m):
    b, seq, _ = q_r.shape
    kb = _row_tile(seq, MLA_KEY_BLOCK)
    tq = _row_tile(seq, MLA_Q_ROWS if kb == seq else MLA_Q_ROWS // 2)
    pair = 2 * LANES
    npair = MLA_PAIRS_PER_STEP
    real = lambda n: pl.BlockSpec((1, seq, n), lambda i, j: (i, 0, j))
    meta = lambda n: pl.BlockSpec((1, N_META, n), lambda i, j: (i, 0, j))
    return pl.pallas_call(
        functools.partial(_mla_attn_kernel, tq=tq, kb=kb),
        grid=(b, MLA_HEADS // 2 // npair),
        in_specs=[real(npair * pair)] * 3 + [meta(npair * pair)] * 3,
        out_specs=[real(npair * LANES), meta(npair * LANES)],
        out_shape=[jax.ShapeDtypeStruct((b, seq, MLA_HEADS * MLA_V), BF16),
                   jax.ShapeDtypeStruct((b, N_META, MLA_HEADS * MLA_V), BF16)],
        scratch_shapes=[pltpu.VMEM((npair, LANES, pair), BF16),
                        pltpu.VMEM((npair, LANES, pair), BF16), pltpu.VMEM((npair, 2 * N_META, pair), BF16),
                        pltpu.VMEM((2, tq, pair), F32), pltpu.VMEM((2, tq, 1), F32)],
        compiler_params=pltpu.CompilerParams(dimension_semantics=("parallel", "parallel"),
                                             vmem_limit_bytes=VMEM_LIMIT_BYTES),
        name="mla_attn",
    )(q_r, k_r, v_r, q_m, k_m, v_m)


POST_TILE_ROWS = 512
POST_SUB_ROWS = 256
POST_FF_CHUNK = 1024


def _layer_norm(x, g, b):
    mu = jnp.mean(x, axis=-1, keepdims=True)
    xc = x - mu
    var = jnp.mean(xc * xc, axis=-1, keepdims=True)
    return xc * lax.rsqrt(var + LN_EPS) * g + b


def _post_kernel(a_ref, h_ref, wo_ref, g1_ref, b1_ref, w1_ref, w2_ref, g2_ref, b2_ref, o_ref, *,
                 alpha, ff_chunk, sub_rows):
    subs = [slice(r * sub_rows, (r + 1) * sub_rows) for r in range(a_ref.shape[0] // sub_rows)]
    mix_next = _dot(a_ref[subs[0], :], wo_ref[...])
    pending = None
    for r, rows in enumerate(subs):
        mix = mix_next
        if r + 1 < len(subs):
            mix_next = _dot(a_ref[subs[r + 1], :], wo_ref[...])
        h1 = _layer_norm(alpha * h_ref[rows, :] + mix, g1_ref[...], b1_ref[...])
        h1b = h1.astype(BF16)
        acc = jnp.zeros(h1.shape, F32)
        for c in range(D_FF // ff_chunk):
            sl = slice(c * ff_chunk, (c + 1) * ff_chunk)
            u = jnp.maximum(_dot(h1b, w1_ref[:, sl]), 0.0)
            acc = acc + _dot((u * u).astype(BF16), w2_ref[sl, :])
        if pending is not None:
            p_rows, p_h1, p_acc = pending
            o_ref[p_rows, :] = _layer_norm(alpha * p_h1 + p_acc, g2_ref[...], b2_ref[...])
        pending = (rows, h1, acc)
    p_rows, p_h1, p_acc = pending
    o_ref[p_rows, :] = _layer_norm(alpha * p_h1 + p_acc, g2_ref[...], b2_ref[...])


def _post(a2d, h2d, w, alpha, tm):
    rows = a2d.shape[0]
    row_spec = pl.BlockSpec((tm, D_MODEL), lambda i: (i, 0))
    names = ('wo', 'g1', 'b1', 'w1', 'w2', 'g2', 'b2')
    return pl.pallas_call(
        functools.partial(_post_kernel, alpha=alpha, ff_chunk=POST_FF_CHUNK, sub_rows=min(tm, POST_SUB_ROWS)),
        grid=(rows // tm,),
        in_specs=[row_spec, row_spec] + [_const_spec(w[n].shape) for n in names],
        out_specs=row_spec,
        out_shape=jax.ShapeDtypeStruct((rows, D_MODEL), F32),
        compiler_params=pltpu.CompilerParams(dimension_semantics=("parallel",),
                                             vmem_limit_bytes=VMEM_LIMIT_BYTES),
        name="post_mlp",
    )(a2d, h2d, *[w[n] for n in names])


def _rope_block(blk, c, s1, s2):
    return blk * c + pltpu.roll(blk, LANES - GQA_ROT // 2, 1) * s1 + pltpu.roll(blk, GQA_ROT // 2, 1) * s2


def _gqa_proj_kernel(*refs, with_q):
    if with_q:
        h_ref, kc_ref, ks1_ref, ks2_ref, wk_ref, wv_ref, qc_ref, qs1_ref, qs2_ref, wq_ref, k_ref, v_ref, q_ref = refs
    else:
        h_ref, kc_ref, ks1_ref, ks2_ref, wk_ref, wv_ref, k_ref, v_ref = refs
    hb = h_ref[...].astype(BF16)
    if with_q:
        q = _dot(hb, wq_ref[...])
        qc, qs1, qs2 = qc_ref[...], qs1_ref[...], qs2_ref[...]
        for j in range(q.shape[1] // LANES):
            sl = slice(j * LANES, (j + 1) * LANES)
            q_ref[:, sl] = _rope_block(q[:, sl], qc, qs1, qs2).astype(BF16)
    k = _dot(hb, wk_ref[...])
    v = _dot(hb, wv_ref[...])
    kc, ks1, ks2 = kc_ref[...], ks1_ref[...], ks2_ref[...]
    first = lax.broadcasted_iota(jnp.int32, (1, LANES), 1) < GQA_HEAD_DIM
    ones = jnp.ones((k.shape[0], LANES), BF16)
    for j in range(GQA_KV_HEADS // 2):
        sl = slice(j * LANES, (j + 1) * LANES)
        kb = _rope_block(k[:, sl], kc, ks1, ks2)
        vb = v[:, sl]
        ksw = pltpu.roll(kb, GQA_HEAD_DIM, 1)
        vsw = pltpu.roll(vb, GQA_HEAD_DIM, 1)
        for g, (kx, ky, vx, vy) in ((2 * j, (kb, ksw, vb, vsw)), (2 * j + 1, (ksw, kb, vsw, vb))):
            k_ref[:, g * LANES:(g + 1) * LANES] = jnp.where(first, kx, ky).astype(BF16)
            v_ref[:, (2 * g) * LANES:(2 * g + 1) * LANES] = jnp.where(first, vx, vy).astype(BF16)
            v_ref[:, (2 * g + 1) * LANES:(2 * g + 2) * LANES] = ones


def _gqa_proj(h2d, tabs, w, tm, nper, with_q):
    rows = h2d.shape[0]
    row_spec = lambda n: pl.BlockSpec((tm, n), lambda i: (i, 0))
    tab_spec = pl.BlockSpec((tm, LANES), lambda i: (i % nper, 0))
    nq = GQA_Q_HEADS * GQA_HEAD_DIM
    nk = GQA_KV_HEADS * LANES
    nv = GQA_KV_HEADS * 2 * LANES
    args = [h2d, tabs['kc'], tabs['ks1'], tabs['ks2'], w['wk'], w['wv']]
    in_specs = [row_spec(D_MODEL)] + [tab_spec] * 3 + [_const_spec(w['wk'].shape), _const_spec(w['wv'].shape)]
    out_specs = [row_spec(nk), row_spec(nv)]
    out_shape = [jax.ShapeDtypeStruct((rows, nk), BF16), jax.ShapeDtypeStruct((rows, nv), BF16)]
    if with_q:
        args += [tabs['qc'], tabs['qs1'], tabs['qs2'], w['wq']]
        in_specs += [tab_spec] * 3 + [_const_spec(w['wq'].shape)]
        out_specs.append(row_spec(nq))
        out_shape.append(jax.ShapeDtypeStruct((rows, nq), BF16))
    return pl.pallas_call(
        functools.partial(_gqa_proj_kernel, with_q=with_q),
        grid=(rows // tm,),
        in_specs=in_specs,
        out_specs=out_specs,
        out_shape=out_shape,
        compiler_params=pltpu.CompilerParams(dimension_semantics=("parallel",),
                                             vmem_limit_bytes=VMEM_LIMIT_BYTES),
        name="gqa_proj",
    )(*args)


def _gqa_attn_kernel(sink_ref, q_ref, kp_ref, kc_ref, kn_ref, vp_ref, vc_ref, vn_ref, km_ref, vm_ref, o_ref,
                     kw_ref, vw_ref, *, tq):
    i = pl.program_id(1)
    nsub = tq // BLOCK
    nblk = pl.num_programs(1) * nsub
    mrow = (nsub - 1) * BLOCK
    for src, dst in ((km_ref, kw_ref), (vm_ref, vw_ref)):
        dst[mrow:mrow + BLOCK, :] = jnp.zeros((BLOCK, dst.shape[1]), BF16)
        dst[mrow:mrow + N_META, :] = src[0]
    for prv, cur, nxt, dst in ((kp_ref, kc_ref, kn_ref, kw_ref), (vp_ref, vc_ref, vn_ref, vw_ref)):
        dst[0:BLOCK, :] = prv[0]
        dst[BLOCK:mrow, :] = cur[0, 0:mrow - BLOCK, :]
        dst[mrow + BLOCK:tq + 2 * BLOCK, :] = cur[0, mrow - BLOCK:tq, :]
        dst[tq + 2 * BLOCK:tq + 3 * BLOCK, :] = nxt[0]

    span = 4 * BLOCK
    group = GQA_Q_HEADS // GQA_KV_HEADS
    qi = lax.broadcasted_iota(jnp.int32, (BLOCK, 1), 0)
    col = lax.broadcasted_iota(jnp.int32, (1, span), 1)
    col_blk = col // BLOCK
    col_in = col % BLOCK
    first_head = lax.broadcasted_iota(jnp.int32, (1, LANES), 1) < GQA_HEAD_DIM
    head_idx = lax.broadcasted_iota(jnp.int32, (group, 1, 1), 0)
    always = 1 << 30

    def sub_block(s):
        q0 = s * BLOCK
        blk = i * nsub + s
        mpos = nsub - 1 - s
        is_meta = col_blk == mpos
        rel = col_blk - jnp.where(col_blk > mpos, 1, 0)
        dead = ((rel == 0) & (blk == 0)) | ((rel == 2) & (blk == nblk - 1))
        centre = jnp.where(is_meta, 0, (rel - 1) * BLOCK + col_in)
        reach = jnp.where(is_meta, jnp.where(col_in < N_META, always, -1), jnp.where(dead, -1, WINDOW))
        ceiling = jnp.where(jnp.abs(centre - qi) <= reach, jnp.inf, NEG_INF)[None]
        for g in range(GQA_KV_HEADS):
            lhs = []
            for pr in range(group // 2):
                qp = q_ref[0, pl.ds(q0, BLOCK), (2 * g + pr) * LANES:(2 * g + pr + 1) * LANES]
                lhs += [jnp.where(first_head, qp, jnp.zeros_like(qp)), jnp.where(first_head, jnp.zeros_like(qp), qp)]
            lhs = jnp.concatenate(lhs, axis=0)
            sc = _nt_dot(lhs, kw_ref[pl.ds(q0, span), g * LANES:(g + 1) * LANES])
            sc = jnp.minimum(sc.reshape(group, BLOCK, span), ceiling)
            sink = sink_ref[group * g + group - 1]
            for hh in range(group - 2, -1, -1):
                sink = jnp.where(head_idx == hh, sink_ref[group * g + hh], sink)
            m = jnp.maximum(jnp.max(sc, axis=-1, keepdims=True), sink)
            p = jnp.exp2(sc - m).astype(BF16).reshape(group * BLOCK, span)
            o = _dot(p, vw_ref[pl.ds(q0, span), 2 * g * LANES:2 * (g + 1) * LANES])
            o = o.reshape(group, BLOCK, 2 * LANES)
            res = o[:, :, :LANES] / (o[:, :, LANES:] + jnp.exp2(sink - m))
            for pr in range(group // 2):
                o_ref[0, pl.ds(q0, BLOCK), (2 * g + pr) * LANES:(2 * g + pr + 1) * LANES] = jnp.where(
                    first_head, res[2 * pr], res[2 * pr + 1]).astype(BF16)

    for s in range(nsub):
        sub_block(s)


def _gqa_attn(sink2, q, k, v, k_m, v_m):
    b, seq, nq = q.shape
    nk, nv = k.shape[2], v.shape[2]
    tq = 4 * BLOCK
    assert seq % tq == 0
    nsub = tq // BLOCK
    nblk = seq // BLOCK
    tile = lambda n: pl.BlockSpec((1, tq, n), lambda bi, i: (bi, i, 0))
    prev = lambda n: pl.BlockSpec((1, BLOCK, n), lambda bi, i: (bi, jnp.maximum(i * nsub - 1, 0), 0))
    nxt = lambda n: pl.BlockSpec((1, BLOCK, n), lambda bi, i: (bi, jnp.minimum((i + 1) * nsub, nblk - 1), 0))
    meta = lambda n: pl.BlockSpec((1, N_META, n), lambda bi, i: (bi, 0, 0))
    return pl.pallas_call(
        functools.partial(_gqa_attn_kernel, tq=tq),
        grid=(b, seq // tq),
        in_specs=[pl.BlockSpec(memory_space=pltpu.SMEM), tile(nq), prev(nk), tile(nk), nxt(nk),
                  prev(nv), tile(nv), nxt(nv), meta(nk), meta(nv)],
        out_specs=tile(nq),
        out_shape=jax.ShapeDtypeStruct((b, seq, nq), BF16),
        scratch_shapes=[pltpu.VMEM((tq + 3 * BLOCK, nk), BF16), pltpu.VMEM((tq + 3 * BLOCK, nv), BF16)],
        compiler_params=pltpu.CompilerParams(dimension_semantics=("parallel", "parallel"),
                                             vmem_limit_bytes=VMEM_LIMIT_BYTES),
        name="gqa_attn",
    )(sink2, q, k, k, k, v, v, v, k_m, v_m)


def _swap_halves(w):
    half = w.shape[-1] // 2
    return jnp.concatenate([-w[..., half:], w[..., :half]], axis=-1)


def _prep_mla(w_in, g_q, w_uq, g_kv, w_ukv, w_o):
    d = w_in.shape[0]
    z64 = jnp.zeros((d, MLA_NOPE), F32)
    wkr = w_in[:, MLA_Q_RANK + MLA_KV_RANK:]
    wkr_sw = _swap_halves(wkr)
    w_in_x = jnp.concatenate([w_in[:, :MLA_Q_RANK + MLA_KV_RANK], z64, wkr, wkr, z64, wkr_sw, wkr_sw], axis=1)
    wq = w_uq.reshape(MLA_Q_RANK, MLA_HEADS, MLA_NOPE + MLA_ROPE)
    wq = jnp.concatenate([wq, _swap_halves(wq[..., MLA_NOPE:])], axis=-1).reshape(MLA_Q_RANK, MLA_HEADS * LANES)
    assert MLA_NOPE + MLA_V == LANES
    return dict(w_in=w_in_x.astype(BF16), g_q=g_q.reshape(1, -1), g_kv=g_kv.reshape(1, -1),
                wq=wq.astype(BF16), wkv=w_ukv.astype(BF16)), w_o.astype(BF16)


def _prep_gqa(w_qkv, w_o):
    qd = GQA_Q_HEADS * GQA_HEAD_DIM
    kd = GQA_KV_HEADS * GQA_HEAD_DIM
    return dict(wq=w_qkv[:, :qd].astype(BF16), wk=w_qkv[:, qd:qd + kd].astype(BF16),
                wv=w_qkv[:, qd + kd:].astype(BF16)), w_o.astype(BF16)


def _lane_tables(length, dim, lanes, scale=1.0):
    freq, wc, ws, const = (np.asarray(col) for col in zip(*lanes))
    pos = jnp.arange(length, dtype=F32)
    inv = ROPE_THETA ** (-jnp.arange(0, dim, 2, dtype=F32) / dim)
    ang = pos[:, None] * inv[freq][None, :]
    table = jnp.cos(ang) * wc.astype(np.float32) + jnp.sin(ang) * ws.astype(np.float32) + const.astype(np.float32)
    return table * scale if scale != 1.0 else table


def _mla_tables(length):
    half = MLA_ROPE // 2
    nope = [(0, 0, 0, 0)] * MLA_NOPE
    cos = [(i % half, 1, 0, 0) for i in range(MLA_ROPE)]
    sin = [(i % half, 0, 1, 0) for i in range(MLA_ROPE)]
    scale = (MLA_NOPE + MLA_ROPE) ** -0.5 * LOG2E
    return dict(tq=_lane_tables(length, MLA_ROPE, [(0, 0, 0, 1)] * MLA_NOPE + cos + sin, scale),
                ta=_lane_tables(length, MLA_ROPE, nope + cos + cos),
                tb=_lane_tables(length, MLA_ROPE, nope + sin + sin))


def _gqa_tables(length):
    half = GQA_ROT // 2
    rest = GQA_HEAD_DIM - GQA_ROT
    two = lambda head: head * (LANES // GQA_HEAD_DIM)
    c = two([(i % half, 1, 0, 0) for i in range(GQA_ROT)] + [(0, 0, 0, 1)] * rest)
    s1 = two([(i, 0, -1, 0) for i in range(half)] + [(0, 0, 0, 0)] * (half + rest))
    s2 = two([(0, 0, 0, 0)] * half + [(i, 0, 1, 0) for i in range(half)] + [(0, 0, 0, 0)] * rest)
    scale = GQA_HEAD_DIM ** -0.5 * LOG2E
    tab = lambda lanes, sc=1.0: _lane_tables(length, GQA_ROT, lanes, sc)
    return dict(kc=tab(c), ks1=tab(s1), ks2=tab(s2), qc=tab(c, scale), qs1=tab(s1, scale), qs2=tab(s2, scale))


def _split_tables(tabs):
    return {n: t[N_META:] for n, t in tabs.items()}, {n: t[:N_META] for n, t in tabs.items()}


def _trunk(x, meta_tokens, layers, tables, tile_rows=1024):
    b, seq, d = x.shape
    depth = len(layers)
    alpha = (2.0 * depth) ** 0.25
    h_r = x.reshape(b * seq, d)
    h_m = jnp.broadcast_to(meta_tokens[None].astype(x.dtype), (b, N_META, d)).reshape(b * N_META, d)
    tm_r = _row_tile(seq, tile_rows)
    tm_m = _row_tile(b * N_META, tile_rows)
    np_r = seq // tm_r
    np_m = b * N_META // tm_m
    (mla_r, mla_m), (gqa_r, gqa_m) = (_split_tables(t) for t in tables)
    mla_m, gqa_m = ({n: jnp.tile(t, (b, 1)) for n, t in tm.items()} for tm in (mla_m, gqa_m))
    shp = lambda t, n: t.reshape(b, n, t.shape[-1])
    for i, layer in enumerate(layers):
        last = i == depth - 1
        if layer['kind'] == 'mla':
            q_r, k_r, v_r = _mla_proj(h_r, mla_r, layer['proj'], tm_r, np_r)
            q_m, k_m, v_m = _mla_proj(h_m, mla_m, layer['proj'], tm_m, np_m)
            a_r, a_m = _mla_attn(shp(q_r, seq), shp(k_r, seq), shp(v_r, seq),
                                 shp(q_m, N_META), shp(k_m, N_META), shp(v_m, N_META))
        else:
            assert last, "windowed layer computes real-token outputs only"
            k_r, v_r, q_r = _gqa_proj(h_r, gqa_r, layer['proj'], tm_r, np_r, True)
            k_m, v_m = _gqa_proj(h_m, gqa_m, layer['proj'], tm_m, np_m, False)
            a_r = _gqa_attn(layer['sink2'], shp(q_r, seq), shp(k_r, seq), shp(v_r, seq),
                            shp(k_m, N_META), shp(v_m, N_META))
            a_m = None
        h_r = _post(a_r.reshape(b * seq, d), h_r, layer['post'], alpha, _row_tile(b * seq, POST_TILE_ROWS))
        if not last:
            h_m = _post(a_m.reshape(b * N_META, d), h_m, layer['post'], alpha, tm_m)
    return h_r.reshape(b, seq, d)


def kernel(x_prompt, x_sample, meta_tokens, mla_w_in, mla_g_q, mla_w_uq, mla_g_kv, mla_w_ukv, mla_w_o,
           gqa_w_qkv, gqa_sink, gqa_w_o, mlp_w1, mlp_w2, ln1_g, ln1_b, ln2_g, ln2_b):
    depth = mlp_w1.shape[0]
    layers = []
    for i in range(depth):
        j = i // 2
        if i % 2 == 0:
            proj, wo = _prep_mla(mla_w_in[j], mla_g_q[j], mla_w_uq[j], mla_g_kv[j], mla_w_ukv[j], mla_w_o[j])
            layer = dict(kind='mla', proj=proj)
        else:
            proj, wo = _prep_gqa(gqa_w_qkv[j], gqa_w_o[j])
            layer = dict(kind='gqa', proj=proj, sink2=gqa_sink[j].astype(F32) * LOG2E)
        row = lambda t: t[i].reshape(1, -1).astype(F32)
        layer['post'] = dict(wo=wo, g1=row(ln1_g), b1=row(ln1_b), w1=mlp_w1[i].astype(BF16),
                             w2=mlp_w2[i].astype(BF16), g2=row(ln2_g), b2=row(ln2_b))
        layers.append(layer)
    length = max(x_prompt.shape[1], x_sample.shape[1]) + N_META
    tables = (_mla_tables(length), _gqa_tables(length))
    return (_trunk(x_prompt, meta_tokens, layers, tables), _trunk(x_sample, meta_tokens, layers, tables))
```

```python
import functools
import math

import numpy as np
import jax
import jax.numpy as jnp
from jax import lax
from jax.experimental import pallas as pl
from jax.experimental.pallas import tpu as pltpu

D_MODEL = 1024
N_META = 16
BLOCK = 128
WINDOW = 128
ROPE_THETA = 500000.0
MLA_HEADS = 16
MLA_NOPE = 64
MLA_ROPE = 32
MLA_V = 64
MLA_KV_RANK = 256
MLA_Q_RANK = 768
GQA_Q_HEADS = 16
GQA_KV_HEADS = 4
GQA_HEAD_DIM = 64
GQA_ROT = 16
D_FF = 4096
LN_EPS = 1e-5
RMS_EPS = 1e-6
NEG_INF = -1e30
LOG2E = math.log2(math.e)

LANES = 128
VMEM_LIMIT_BYTES = 56 * 1024 * 1024

BF16 = jnp.bfloat16
F32 = jnp.float32


def _row_tile(rows, target):
    if rows <= target:
        return rows
    t = target
    while rows % t:
        t //= 2
    return t


def _const_spec(shape):
    nd = len(shape)
    return pl.BlockSpec(shape, lambda *_: (0,) * nd, pipeline_mode=pl.Buffered(1))


def _nt_dot(a, b):
    return lax.dot_general(a, b, (((1,), (1,)), ((), ())), preferred_element_type=F32)


def _dot(a, b):
    return jnp.dot(a, b, preferred_element_type=F32)


def _mla_proj_kernel(x_ref, tq_ref, ta_ref, tb_ref, w_in_ref, gq_ref, gkv_ref, wq_ref, wkv_ref,
                     q_ref, k_ref, v_ref):
    xb = x_ref[...].astype(BF16)
    c = _dot(xb, w_in_ref[...])
    cq = c[:, :MLA_Q_RANK]
    cq = cq * lax.rsqrt(jnp.mean(cq * cq, axis=-1, keepdims=True) + RMS_EPS) * gq_ref[...]
    kv0 = MLA_Q_RANK
    ckv = c[:, kv0:kv0 + MLA_KV_RANK]
    ckv = ckv * lax.rsqrt(jnp.mean(ckv * ckv, axis=-1, keepdims=True) + RMS_EPS) * gkv_ref[...]
    r0 = kv0 + MLA_KV_RANK
    kr = c[:, r0:r0 + LANES] * ta_ref[...] + c[:, r0 + LANES:r0 + 2 * LANES] * tb_ref[...]
    q = _dot(cq.astype(BF16), wq_ref[...])
    kv = _dot(ckv.astype(BF16), wkv_ref[...])
    tq = tq_ref[...]
    first = lax.broadcasted_iota(jnp.int32, (1, LANES), 1) < MLA_NOPE
    ones = jnp.ones((kv.shape[0], LANES), BF16)
    for h in range(MLA_HEADS):
        sl = slice(h * LANES, (h + 1) * LANES)
        q_ref[:, sl] = (q[:, sl] * tq).astype(BF16)
        k_ref[:, sl] = jnp.where(first, kv[:, sl], kr).astype(BF16)
    for j in range(MLA_HEADS // 2):
        v_even = pltpu.roll(kv[:, (2 * j) * LANES:(2 * j + 1) * LANES], MLA_V, 1)
        v_odd = kv[:, (2 * j + 1) * LANES:(2 * j + 2) * LANES]
        v_ref[:, (2 * j) * LANES:(2 * j + 1) * LANES] = jnp.where(first, v_even, v_odd).astype(BF16)
        v_ref[:, (2 * j + 1) * LANES:(2 * j + 2) * LANES] = ones


def _mla_proj(x2d, tabs, w, tm, nper):
    rows = x2d.shape[0]
    row_spec = lambda n: pl.BlockSpec((tm, n), lambda i: (i, 0))
    tab_spec = pl.BlockSpec((tm, LANES), lambda i: (i % nper, 0))
    hq = MLA_HEADS * LANES
    return pl.pallas_call(
        _mla_proj_kernel,
        grid=(rows // tm,),
        in_specs=[row_spec(D_MODEL), tab_spec, tab_spec, tab_spec,
                  _const_spec(w['w_in'].shape), _const_spec(w['g_q'].shape), _const_spec(w['g_kv'].shape),
                  _const_spec(w['wq'].shape), _const_spec(w['wkv'].shape)],
        out_specs=[row_spec(hq), row_spec(hq), row_spec(hq)],
        out_shape=[jax.ShapeDtypeStruct((rows, hq), BF16)] * 3,
        compiler_params=pltpu.CompilerParams(dimension_semantics=("parallel",),
                                             vmem_limit_bytes=VMEM_LIMIT_BYTES),
        name="mla_proj",
    )(x2d, tabs['tq'], tabs['ta'], tabs['tb'], w['w_in'], w['g_q'], w['g_kv'], w['wq'], w['wkv'])


MLA_Q_ROWS = 1024
MLA_KEY_BLOCK = 2048


MLA_PAIRS_PER_STEP = 2


def _mla_attn_kernel(qr_ref, kr_ref, vr_ref, qm_ref, km_ref, vm_ref, or_ref, om_ref,
                     vxm_ref, kms_ref, qmx_ref, acc_ref, m_ref, *, tq, kb):
    pair = 2 * LANES
    for pp in range(MLA_PAIRS_PER_STEP):
        wide = lambda ref: ref.at[:, :, pl.ds(pp * pair, pair)]
        narrow = lambda ref: ref.at[:, :, pl.ds(pp * LANES, LANES)]
        _mla_pair_body(wide(qr_ref), wide(kr_ref), wide(vr_ref), wide(qm_ref), wide(km_ref), wide(vm_ref),
                       narrow(or_ref), narrow(om_ref), vxm_ref.at[pp], kms_ref.at[pp], qmx_ref.at[pp],
                       acc_ref, m_ref, tq=tq, kb=kb)


def _mla_pair_body(qr_ref, kr_ref, vr_ref, qm_ref, km_ref, vm_ref, or_ref, om_ref,
                   vxm_ref, kms_ref, qmx_ref, acc_ref, m_ref, *, tq, kb):
    seq = qr_ref.shape[1]
    pair = 2 * LANES
    nkb = seq // kb
    vxm_ref[...] = jnp.zeros(vxm_ref.shape, BF16)
    vxm_ref[:N_META, :] = vm_ref[0]
    kms_ref[...] = jnp.zeros(kms_ref.shape, BF16)
    kms_ref[:N_META, :] = km_ref[0]
    qmx_ref[...] = jnp.zeros(qmx_ref.shape, BF16)
    for t in range(2):
        qmx_ref[t * N_META:(t + 1) * N_META, t * LANES:(t + 1) * LANES] = qm_ref[0, :, t * LANES:(t + 1) * LANES]
    lane = lax.broadcasted_iota(jnp.int32, (1, LANES), 1)
    meta_valid = lane < N_META
    first_head = lane < MLA_V

    def attend(q, feat):
        parts = []
        for j in range(nkb):
            rows = slice(j * kb, (j + 1) * kb)
            s = _nt_dot(q, kr_ref[0, rows, feat])
            m = jnp.max(s, axis=-1, keepdims=True)
            if j == nkb - 1:
                s_m = jnp.where(meta_valid, _nt_dot(q, kms_ref[:, feat]), NEG_INF)
                m = jnp.maximum(m, jnp.max(s_m, axis=-1, keepdims=True))
            o = _dot(jnp.exp2(s - m).astype(BF16), vr_ref[0, rows, :])
            if j == nkb - 1:
                o = o + _dot(jnp.exp2(s_m - m).astype(BF16), vxm_ref[...])
            parts.append((m, o))
        m_all, o = parts[0]
        if nkb > 1:
            for m_j, _ in parts[1:]:
                m_all = jnp.maximum(m_all, m_j)
            o = sum(o_j * jnp.exp2(m_j - m_all) for m_j, o_j in parts)
        return o[:, :LANES] / o[:, LANES:]

    def chunk(r0):
        res = [attend(qr_ref[0, r0:r0 + tq, t * LANES:(t + 1) * LANES], slice(t * LANES, (t + 1) * LANES))
               for t in range(2)]
        or_ref[0, r0:r0 + tq, :] = jnp.where(first_head, res[0], res[1]).astype(BF16)

    def chunk_block(it, carry):
        ci, j = it // nkb, it % nkb
        r0 = pl.multiple_of(ci * tq, tq)
        k0 = pl.multiple_of(j * kb, kb)
        first_block = j == 0
        with_meta = jnp.where(j == nkb - 1, 1.0, 0.0)
        res = []
        for t in range(2):
            feat = slice(t * LANES, (t + 1) * LANES)
            q = qr_ref[0, pl.ds(r0, tq), feat]
            s = _nt_dot(q, kr_ref[0, pl.ds(k0, kb), feat])
            s_m = jnp.where(meta_valid, _nt_dot(q, kms_ref[:, feat]), NEG_INF)
            m_prev = jnp.where(first_block, NEG_INF, m_ref[t])
            m = jnp.maximum(jnp.maximum(m_prev, jnp.max(s, axis=-1, keepdims=True)),
                            jnp.max(s_m, axis=-1, keepdims=True))
            o = (acc_ref[t] * jnp.exp2(m_prev - m) + _dot(jnp.exp2(s - m).astype(BF16), vr_ref[0, pl.ds(k0, kb), :]) +
                 _dot((jnp.exp2(s_m - m) * with_meta).astype(BF16), vxm_ref[...]))
            acc_ref[t] = o
            m_ref[t] = m
            res.append(o[:, :LANES] / o[:, LANES:])
        or_ref[0, pl.ds(r0, tq), :] = jnp.where(first_head, res[0], res[1]).astype(BF16)
        return carry

    if nkb == 1:
        for r0 in range(0, seq, tq):
            chunk(r0)
    else:
        acc_ref[...] = jnp.zeros(acc_ref.shape, F32)
        m_ref[...] = jnp.full(m_ref.shape, NEG_INF, F32)
        lax.fori_loop(0, (seq // tq) * nkb, chunk_block, 0)
    res = attend(qmx_ref[...], slice(0, pair))
    om_ref[0] = jnp.where(first_head, res[:N_META], res[N_META:]).astype(BF16)


def _mla_attn(q_r, k_r, v_r, q_m, k_m, v_m):
    b, seq, _ = q_r.shape
    kb = _row_tile(seq, MLA_KEY_BLOCK)
    tq = _row_tile(seq, MLA_Q_ROWS if kb == seq else MLA_Q_ROWS // 2)
    pair = 2 * LANES
    npair = MLA_PAIRS_PER_STEP
    real = lambda n: pl.BlockSpec((1, seq, n), lambda i, j: (i, 0, j))
    meta = lambda n: pl.BlockSpec((1, N_META, n), lambda i, j: (i, 0, j))
    return pl.pallas_call(
        functools.partial(_mla_attn_kernel, tq=tq, kb=kb),
        grid=(b, MLA_HEADS // 2 // npair),
        in_specs=[real(npair * pair)] * 3 + [meta(npair * pair)] * 3,
        out_specs=[real(npair * LANES), meta(npair * LANES)],
        out_shape=[jax.ShapeDtypeStruct((b, seq, MLA_HEADS * MLA_V), BF16),
                   jax.ShapeDtypeStruct((b, N_META, MLA_HEADS * MLA_V), BF16)],
        scratch_shapes=[pltpu.VMEM((npair, LANES, pair), BF16),
                        pltpu.VMEM((npair, LANES, pair), BF16), pltpu.VMEM((npair, 2 * N_META, pair), BF16),
                        pltpu.VMEM((2, tq, pair), F32), pltpu.VMEM((2, tq, 1), F32)],
        compiler_params=pltpu.CompilerParams(dimension_semantics=("parallel", "parallel"),
                                             vmem_limit_bytes=VMEM_LIMIT_BYTES),
        name="mla_attn",
    )(q_r, k_r, v_r, q_m, k_m, v_m)


POST_TILE_ROWS = 512
POST_SUB_ROWS = 256
POST_FF_CHUNK = 1024


def _layer_norm(x, g, b):
    mu = jnp.mean(x, axis=-1, keepdims=True)
    xc = x - mu
    var = jnp.mean(xc * xc, axis=-1, keepdims=True)
    return xc * lax.rsqrt(var + LN_EPS) * g + b


def _post_kernel(a_ref, h_ref, wo_ref, g1_ref, b1_ref, w1_ref, w2_ref, g2_ref, b2_ref, o_ref, *,
                 alpha, ff_chunk, sub_rows):
    subs = [slice(r * sub_rows, (r + 1) * sub_rows) for r in range(a_ref.shape[0] // sub_rows)]
    mix_next = _dot(a_ref[subs[0], :], wo_ref[...])
    pending = None
    for r, rows in enumerate(subs):
        mix = mix_next
        if r + 1 < len(subs):
            mix_next = _dot(a_ref[subs[r + 1], :], wo_ref[...])
        h1 = _layer_norm(alpha * h_ref[rows, :] + mix, g1_ref[...], b1_ref[...])
        h1b = h1.astype(BF16)
        acc = jnp.zeros(h1.shape, F32)
        for c in range(D_FF // ff_chunk):
            sl = slice(c * ff_chunk, (c + 1) * ff_chunk)
            u = jnp.maximum(_dot(h1b, w1_ref[:, sl]), 0.0)
            acc = acc + _dot((u * u).astype(BF16), w2_ref[sl, :])
        if pending is not None:
            p_rows, p_h1, p_acc = pending
            o_ref[p_rows, :] = _layer_norm(alpha * p_h1 + p_acc, g2_ref[...], b2_ref[...])
        pending = (rows, h1, acc)
    p_rows, p_h1, p_acc = pending
    o_ref[p_rows, :] = _layer_norm(alpha * p_h1 + p_acc, g2_ref[...], b2_ref[...])


def _post(a2d, h2d, w, alpha, tm):
    rows = a2d.shape[0]
    row_spec = pl.BlockSpec((tm, D_MODEL), lambda i: (i, 0))
    names = ('wo', 'g1', 'b1', 'w1', 'w2', 'g2', 'b2')
    return pl.pallas_call(
        functools.partial(_post_kernel, alpha=alpha, ff_chunk=POST_FF_CHUNK, sub_rows=min(tm, POST_SUB_ROWS)),
        grid=(rows // tm,),
        in_specs=[row_spec, row_spec] + [_const_spec(w[n].shape) for n in names],
        out_specs=row_spec,
        out_shape=jax.ShapeDtypeStruct((rows, D_MODEL), F32),
        compiler_params=pltpu.CompilerParams(dimension_semantics=("parallel",),
                                             vmem_limit_bytes=VMEM_LIMIT_BYTES),
        name="post_mlp",
    )(a2d, h2d, *[w[n] for n in names])


def _rope_block(blk, c, s1, s2):
    return blk * c + pltpu.roll(blk, LANES - GQA_ROT // 2, 1) * s1 + pltpu.roll(blk, GQA_ROT // 2, 1) * s2


def _gqa_proj_kernel(*refs, with_q):
    if with_q:
        h_ref, kc_ref, ks1_ref, ks2_ref, wk_ref, wv_ref, qc_ref, qs1_ref, qs2_ref, wq_ref, k_ref, v_ref, q_ref = refs
    else:
        h_ref, kc_ref, ks1_ref, ks2_ref, wk_ref, wv_ref, k_ref, v_ref = refs
    hb = h_ref[...].astype(BF16)
    if with_q:
        q = _dot(hb, wq_ref[...])
        qc, qs1, qs2 = qc_ref[...], qs1_ref[...], qs2_ref[...]
        for j in range(q.shape[1] // LANES):
            sl = slice(j * LANES, (j + 1) * LANES)
            q_ref[:, sl] = _rope_block(q[:, sl], qc, qs1, qs2).astype(BF16)
    k = _dot(hb, wk_ref[...])
    v = _dot(hb, wv_ref[...])
    kc, ks1, ks2 = kc_ref[...], ks1_ref[...], ks2_ref[...]
    first = lax.broadcasted_iota(jnp.int32, (1, LANES), 1) < GQA_HEAD_DIM
    ones = jnp.ones((k.shape[0], LANES), BF16)
    for j in range(GQA_KV_HEADS // 2):
        sl = slice(j * LANES, (j + 1) * LANES)
        kb = _rope_block(k[:, sl], kc, ks1, ks2)
        vb = v[:, sl]
        ksw = pltpu.roll(kb, GQA_HEAD_DIM, 1)
        vsw = pltpu.roll(vb, GQA_HEAD_DIM, 1)
        for g, (kx, ky, vx, vy) in ((2 * j, (kb, ksw, vb, vsw)), (2 * j + 1, (ksw, kb, vsw, vb))):
            k_ref[:, g * LANES:(g + 1) * LANES] = jnp.where(first, kx, ky).astype(BF16)
            v_ref[:, (2 * g) * LANES:(2 * g + 1) * LANES] = jnp.where(first, vx, vy).astype(BF16)
            v_ref[:, (2 * g + 1) * LANES:(2 * g + 2) * LANES] = ones


def _gqa_proj(h2d, tabs, w, tm, nper, with_q):
    rows = h2d.shape[0]
    row_spec = lambda n: pl.BlockSpec((tm, n), lambda i: (i, 0))
    tab_spec = pl.BlockSpec((tm, LANES), lambda i: (i % nper, 0))
    nq = GQA_Q_HEADS * GQA_HEAD_DIM
    nk = GQA_KV_HEADS * LANES
    nv = GQA_KV_HEADS * 2 * LANES
    args = [h2d, tabs['kc'], tabs['ks1'], tabs['ks2'], w['wk'], w['wv']]
    in_specs = [row_spec(D_MODEL)] + [tab_spec] * 3 + [_const_spec(w['wk'].shape), _const_spec(w['wv'].shape)]
    out_specs = [row_spec(nk), row_spec(nv)]
    out_shape = [jax.ShapeDtypeStruct((rows, nk), BF16), jax.ShapeDtypeStruct((rows, nv), BF16)]
    if with_q:
        args += [tabs['qc'], tabs['qs1'], tabs['qs2'], w['wq']]
        in_specs += [tab_spec] * 3 + [_const_spec(w['wq'].shape)]
        out_specs.append(row_spec(nq))
        out_shape.append(jax.ShapeDtypeStruct((rows, nq), BF16))
    return pl.pallas_call(
        functools.partial(_gqa_proj_kernel, with_q=with_q),
        grid=(rows // tm,),
        in_specs=in_specs,
        out_specs=out_specs,
        out_shape=out_shape,
        compiler_params=pltpu.CompilerParams(dimension_semantics=("parallel",),
                                             vmem_limit_bytes=VMEM_LIMIT_BYTES),
        name="gqa_proj",
    )(*args)


GQA_TILES_PER_STEP = 2


def _gqa_attn_kernel(sink_ref, q_ref, kp_ref, kc_ref, kn_ref, vp_ref, vc_ref, vn_ref, km_ref, vm_ref, o_ref,
                     kw_ref, vw_ref, *, tq):
    ntile = GQA_TILES_PER_STEP
    for h in range(ntile):
        rows = pl.ds(h * tq, tq)
        before = pl.ds(h * tq - BLOCK, BLOCK)
        after = pl.ds((h + 1) * tq, BLOCK)
        first, last = h == 0, h == ntile - 1
        _gqa_tile_body(sink_ref, q_ref.at[:, rows],
                       kp_ref if first else kc_ref.at[:, before], kc_ref.at[:, rows],
                       kn_ref if last else kc_ref.at[:, after],
                       vp_ref if first else vc_ref.at[:, before], vc_ref.at[:, rows],
                       vn_ref if last else vc_ref.at[:, after],
                       km_ref, vm_ref, o_ref.at[:, rows], kw_ref.at[h], vw_ref.at[h],
                       tile=pl.program_id(1) * ntile + h, tiles=pl.num_programs(1) * ntile, tq=tq)


def _gqa_tile_body(sink_ref, q_ref, kp_ref, kc_ref, kn_ref, vp_ref, vc_ref, vn_ref, km_ref, vm_ref, o_ref,
                   kw_ref, vw_ref, *, tile, tiles, tq):
    i = tile
    nsub = tq // BLOCK
    nblk = tiles * nsub
    mrow = (nsub - 1) * BLOCK
    for src, dst in ((km_ref, kw_ref), (vm_ref, vw_ref)):
        dst[mrow:mrow + BLOCK, :] = jnp.zeros((BLOCK, dst.shape[1]), BF16)
        dst[mrow:mrow + N_META, :] = src[0]
    for prv, cur, nxt, dst in ((kp_ref, kc_ref, kn_ref, kw_ref), (vp_ref, vc_ref, vn_ref, vw_ref)):
        dst[0:BLOCK, :] = prv[0]
        dst[BLOCK:mrow, :] = cur[0, 0:mrow - BLOCK, :]
        dst[mrow + BLOCK:tq + 2 * BLOCK, :] = cur[0, mrow - BLOCK:tq, :]
        dst[tq + 2 * BLOCK:tq + 3 * BLOCK, :] = nxt[0]

    span = 4 * BLOCK
    group = GQA_Q_HEADS // GQA_KV_HEADS
    qi = lax.broadcasted_iota(jnp.int32, (BLOCK, 1), 0)
    col = lax.broadcasted_iota(jnp.int32, (1, span), 1)
    col_blk = col // BLOCK
    col_in = col % BLOCK
    first_head = lax.broadcasted_iota(jnp.int32, (1, LANES), 1) < GQA_HEAD_DIM
    head_idx = lax.broadcasted_iota(jnp.int32, (group, 1, 1), 0)
    always = 1 << 30

    def sub_block(s):
        q0 = s * BLOCK
        blk = i * nsub + s
        mpos = nsub - 1 - s
        is_meta = col_blk == mpos
        rel = col_blk - jnp.where(col_blk > mpos, 1, 0)
        dead = ((rel == 0) & (blk == 0)) | ((rel == 2) & (blk == nblk - 1))
        centre = jnp.where(is_meta, 0, (rel - 1) * BLOCK + col_in)
        reach = jnp.where(is_meta, jnp.where(col_in < N_META, always, -1), jnp.where(dead, -1, WINDOW))
        ceiling = jnp.where(jnp.abs(centre - qi) <= reach, jnp.inf, NEG_INF)[None]
        for g in range(GQA_KV_HEADS):
            lhs = []
            for pr in range(group // 2):
                qp = q_ref[0, pl.ds(q0, BLOCK), (2 * g + pr) * LANES:(2 * g + pr + 1) * LANES]
                lhs += [jnp.where(first_head, qp, jnp.zeros_like(qp)), jnp.where(first_head, jnp.zeros_like(qp), qp)]
            lhs = jnp.concatenate(lhs, axis=0)
            sc = _nt_dot(lhs, kw_ref[pl.ds(q0, span), g * LANES:(g + 1) * LANES])
            sc = jnp.minimum(sc.reshape(group, BLOCK, span), ceiling)
            sink = sink_ref[group * g + group - 1]
            for hh in range(group - 2, -1, -1):
                sink = jnp.where(head_idx == hh, sink_ref[group * g + hh], sink)
            m = jnp.maximum(jnp.max(sc, axis=-1, keepdims=True), sink)
            p = jnp.exp2(sc - m).astype(BF16).reshape(group * BLOCK, span)
            o = _dot(p, vw_ref[pl.ds(q0, span), 2 * g * LANES:2 * (g + 1) * LANES])
            o = o.reshape(group, BLOCK, 2 * LANES)
            res = o[:, :, :LANES] / (o[:, :, LANES:] + jnp.exp2(sink - m))
            for pr in range(group // 2):
                o_ref[0, pl.ds(q0, BLOCK), (2 * g + pr) * LANES:(2 * g + pr + 1) * LANES] = jnp.where(
                    first_head, res[2 * pr], res[2 * pr + 1]).astype(BF16)

    for s in range(nsub):
        sub_block(s)


def _gqa_attn(sink2, q, k, v, k_m, v_m):
    b, seq, nq = q.shape
    nk, nv = k.shape[2], v.shape[2]
    tq = 4 * BLOCK
    rows = GQA_TILES_PER_STEP * tq
    assert seq % rows == 0
    nsub = rows // BLOCK
    nblk = seq // BLOCK
    tile = lambda n: pl.BlockSpec((1, rows, n), lambda bi, i: (bi, i, 0))
    prev = lambda n: pl.BlockSpec((1, BLOCK, n), lambda bi, i: (bi, jnp.maximum(i * nsub - 1, 0), 0))
    nxt = lambda n: pl.BlockSpec((1, BLOCK, n), lambda bi, i: (bi, jnp.minimum((i + 1) * nsub, nblk - 1), 0))
    meta = lambda n: pl.BlockSpec((1, N_META, n), lambda bi, i: (bi, 0, 0))
    return pl.pallas_call(
        functools.partial(_gqa_attn_kernel, tq=tq),
        grid=(b, seq // rows),
        in_specs=[pl.BlockSpec(memory_space=pltpu.SMEM), tile(nq), prev(nk), tile(nk), nxt(nk),
                  prev(nv), tile(nv), nxt(nv), meta(nk), meta(nv)],
        out_specs=tile(nq),
        out_shape=jax.ShapeDtypeStruct((b, seq, nq), BF16),
        scratch_shapes=[pltpu.VMEM((GQA_TILES_PER_STEP, tq + 3 * BLOCK, nk), BF16),
                        pltpu.VMEM((GQA_TILES_PER_STEP, tq + 3 * BLOCK, nv), BF16)],
        compiler_params=pltpu.CompilerParams(dimension_semantics=("parallel", "parallel"),
                                             vmem_limit_bytes=VMEM_LIMIT_BYTES),
        name="gqa_attn",
    )(sink2, q, k, k, k, v, v, v, k_m, v_m)


def _swap_halves(w):
    half = w.shape[-1] // 2
    return jnp.concatenate([-w[..., half:], w[..., :half]], axis=-1)


def _prep_mla(w_in, g_q, w_uq, g_kv, w_ukv, w_o):
    d = w_in.shape[0]
    z64 = jnp.zeros((d, MLA_NOPE), F32)
    wkr = w_in[:, MLA_Q_RANK + MLA_KV_RANK:]
    wkr_sw = _swap_halves(wkr)
    w_in_x = jnp.concatenate([w_in[:, :MLA_Q_RANK + MLA_KV_RANK], z64, wkr, wkr, z64, wkr_sw, wkr_sw], axis=1)
    wq = w_uq.reshape(MLA_Q_RANK, MLA_HEADS, MLA_NOPE + MLA_ROPE)
    wq = jnp.concatenate([wq, _swap_halves(wq[..., MLA_NOPE:])], axis=-1).reshape(MLA_Q_RANK, MLA_HEADS * LANES)
    assert MLA_NOPE + MLA_V == LANES
    return dict(w_in=w_in_x.astype(BF16), g_q=g_q.reshape(1, -1), g_kv=g_kv.reshape(1, -1),
                wq=wq.astype(BF16), wkv=w_ukv.astype(BF16)), w_o.astype(BF16)


def _prep_gqa(w_qkv, w_o):
    qd = GQA_Q_HEADS * GQA_HEAD_DIM
    kd = GQA_KV_HEADS * GQA_HEAD_DIM
    return dict(wq=w_qkv[:, :qd].astype(BF16), wk=w_qkv[:, qd:qd + kd].astype(BF16),
                wv=w_qkv[:, qd + kd:].astype(BF16)), w_o.astype(BF16)


def _lane_tables(length, dim, lanes, scale=1.0):
    freq, wc, ws, const = (np.asarray(col) for col in zip(*lanes))
    pos = jnp.arange(length, dtype=F32)
    inv = ROPE_THETA ** (-jnp.arange(0, dim, 2, dtype=F32) / dim)
    ang = pos[:, None] * inv[freq][None, :]
    table = jnp.cos(ang) * wc.astype(np.float32) + jnp.sin(ang) * ws.astype(np.float32) + const.astype(np.float32)
    return table * scale if scale != 1.0 else table


def _mla_tables(length):
    half = MLA_ROPE // 2
    nope = [(0, 0, 0, 0)] * MLA_NOPE
    cos = [(i % half, 1, 0, 0) for i in range(MLA_ROPE)]
    sin = [(i % half, 0, 1, 0) for i in range(MLA_ROPE)]
    scale = (MLA_NOPE + MLA_ROPE) ** -0.5 * LOG2E
    return dict(tq=_lane_tables(length, MLA_ROPE, [(0, 0, 0, 1)] * MLA_NOPE + cos + sin, scale),
                ta=_lane_tables(length, MLA_ROPE, nope + cos + cos),
                tb=_lane_tables(length, MLA_ROPE, nope + sin + sin))


def _gqa_tables(length):
    half = GQA_ROT // 2
    rest = GQA_HEAD_DIM - GQA_ROT
    two = lambda head: head * (LANES // GQA_HEAD_DIM)
    c = two([(i % half, 1, 0, 0) for i in range(GQA_ROT)] + [(0, 0, 0, 1)] * rest)
    s1 = two([(i, 0, -1, 0) for i in range(half)] + [(0, 0, 0, 0)] * (half + rest))
    s2 = two([(0, 0, 0, 0)] * half + [(i, 0, 1, 0) for i in range(half)] + [(0, 0, 0, 0)] * rest)
    scale = GQA_HEAD_DIM ** -0.5 * LOG2E
    tab = lambda lanes, sc=1.0: _lane_tables(length, GQA_ROT, lanes, sc)
    return dict(kc=tab(c), ks1=tab(s1), ks2=tab(s2), qc=tab(c, scale), qs1=tab(s1, scale), qs2=tab(s2, scale))


def _split_tables(tabs):
    return {n: t[N_META:] for n, t in tabs.items()}, {n: t[:N_META] for n, t in tabs.items()}


def _trunk(x, meta_tokens, layers, tables, tile_rows=1024):
    b, seq, d = x.shape
    depth = len(layers)
    alpha = (2.0 * depth) ** 0.25
    h_r = x.reshape(b * seq, d)
    h_m = jnp.broadcast_to(meta_tokens[None].astype(x.dtype), (b, N_META, d)).reshape(b * N_META, d)
    tm_r = _row_tile(seq, tile_rows)
    tm_m = _row_tile(b * N_META, tile_rows)
    np_r = seq // tm_r
    np_m = b * N_META // tm_m
    (mla_r, mla_m), (gqa_r, gqa_m) = (_split_tables(t) for t in tables)
    mla_m, gqa_m = ({n: jnp.tile(t, (b, 1)) for n, t in tm.items()} for tm in (mla_m, gqa_m))
    shp = lambda t, n: t.reshape(b, n, t.shape[-1])
    for i, layer in enumerate(layers):
        last = i == depth - 1
        if layer['kind'] == 'mla':
            q_r, k_r, v_r = _mla_proj(h_r, mla_r, layer['proj'], tm_r, np_r)
            q_m, k_m, v_m = _mla_proj(h_m, mla_m, layer['proj'], tm_m, np_m)
            a_r, a_m = _mla_attn(shp(q_r, seq), shp(k_r, seq), shp(v_r, seq),
                                 shp(q_m, N_META), shp(k_m, N_META), shp(v_m, N_META))
        else:
            assert last, "windowed layer computes real-token outputs only"
            k_r, v_r, q_r = _gqa_proj(h_r, gqa_r, layer['proj'], tm_r, np_r, True)
            k_m, v_m = _gqa_proj(h_m, gqa_m, layer['proj'], tm_m, np_m, False)
            a_r = _gqa_attn(layer['sink2'], shp(q_r, seq), shp(k_r, seq), shp(v_r, seq),
                            shp(k_m, N_META), shp(v_m, N_META))
            a_m = None
        h_r = _post(a_r.reshape(b * seq, d), h_r, layer['post'], alpha, _row_tile(b * seq, POST_TILE_ROWS))
        if not last:
            h_m = _post(a_m.reshape(b * N_META, d), h_m, layer['post'], alpha, tm_m)
    return h_r.reshape(b, seq, d)


def kernel(x_prompt, x_sample, meta_tokens, mla_w_in, mla_g_q, mla_w_uq, mla_g_kv, mla_w_ukv, mla_w_o,
           gqa_w_qkv, gqa_sink, gqa_w_o, mlp_w1, mlp_w2, ln1_g, ln1_b, ln2_g, ln2_b):
    depth = mlp_w1.shape[0]
    layers = []
    for i in range(depth):
        j = i // 2
        if i % 2 == 0:
            proj, wo = _prep_mla(mla_w_in[j], mla_g_q[j], mla_w_uq[j], mla_g_kv[j], mla_w_ukv[j], mla_w_o[j])
            layer = dict(kind='mla', proj=proj)
        else:
            proj, wo = _prep_gqa(gqa_w_qkv[j], gqa_w_o[j])
            layer = dict(kind='gqa', proj=proj, sink2=gqa_sink[j].astype(F32) * LOG2E)
        row = lambda t: t[i].reshape(1, -1).astype(F32)
        layer['post'] = dict(wo=wo, g1=row(ln1_g), b1=row(ln1_b), w1=mlp_w1[i].astype(BF16),
                             w2=mlp_w2[i].astype(BF16), g2=row(ln2_g), b2=row(ln2_b))
        layers.append(layer)
    length = max(x_prompt.shape[1], x_sample.shape[1]) + N_META
    tables = (_mla_tables(length), _gqa_tables(length))
    return (_trunk(x_prompt, meta_tokens, layers, tables), _trunk(x_sample, meta_tokens, layers, tables))
```
